```python
import jax
import jax.numpy as jnp
from jax import lax
import numpy as np

D_MODEL = 2048
BATCH = 4
SEQ = 4096
DEPTH = 1

N_HEADS = 16
N_KV_GROUPS = 4
HEADS_PER_GROUP = N_HEADS // N_KV_GROUPS
HEAD_DIM = 128
L_CMP = 32
STRIDE_CMP = 16
CMP_HIDDEN = HEAD_DIM
L_SEL = 64
N_SEL = 16
WINDOW = 512
Q_BLOCK = 32
CONV_CH = D_MODEL // 2
CONV_WIDTH = 31
PEER_HEADS = 8
PEER_NKEYS = 128
PEER_EXPERTS = PEER_NKEYS * PEER_NKEYS
PEER_QDIM = 256
PEER_HALF = PEER_QDIM // 2
PEER_TOPK = 16
PEER_CHUNK = 128

EPS = 1e-6
NEG_BIG = 1e30

Q_COLS = N_HEADS * HEAD_DIM
KV_COLS = N_KV_GROUPS * HEAD_DIM
IN_SPLIT_SIZES = (Q_COLS,) + (KV_COLS,) * 6 + (3 * N_HEADS, 2 * CONV_CH, 2 * D_MODEL)
IN_COLS = sum(IN_SPLIT_SIZES)
IN_SPLIT_POINTS = tuple(int(v) for v in np.cumsum(IN_SPLIT_SIZES)[:-1])

kernel_name = 'hybrid_nsa_conformer_peer_adaln'


def rms_norm(x, g):
    xf = x.astype(jnp.float32)
    y = xf * lax.rsqrt(jnp.mean(xf * xf, axis=-1, keepdims=True) + EPS)
    return (y * g).astype(x.dtype)


def layer_norm(x, g, b):
    xf = x.astype(jnp.float32)
    mu = jnp.mean(xf, axis=-1, keepdims=True)
    xc = xf - mu
    y = xc * lax.rsqrt(jnp.mean(xc * xc, axis=-1, keepdims=True) + EPS)
    return (y * g + b).astype(x.dtype)


def masked_softmax(s, mask):
    s = jnp.where(mask, s.astype(jnp.float32), -NEG_BIG)
    p = jax.nn.softmax(s, axis=-1)
    return jnp.where(mask, p, 0.0)


def compress_blocks(kv, pos, w1, w2):
    B, S, G, dk = kv.shape
    n_cmp = (S - L_CMP) // STRIDE_CMP + 1
    idx = jnp.arange(n_cmp)[:, None] * STRIDE_CMP + jnp.arange(L_CMP)[None, :]
    blocks = kv[:, idx] + pos[None, None, :, None, :]
    blocks = blocks.transpose(0, 1, 3, 2, 4).reshape(B, n_cmp, G, L_CMP * dk)
    return jax.nn.gelu(blocks @ w1) @ w2


def nsa_attention(q, k_cmp, v_cmp, k_slc, v_slc, k_win, v_win, gates):
    B, S, G, HPG, DK = q.shape
    scale = DK ** -0.5
    n_cmp = k_cmp.shape[1]
    n_blk = S // L_SEL
    n_sel = min(N_SEL, n_blk)
    n_q = S // Q_BLOCK
    cmp_start = jnp.arange(n_cmp) * STRIDE_CMP
    cmp_end = cmp_start + L_CMP - 1
    blk_start = jnp.arange(n_blk) * L_SEL
    overlap = ((cmp_start[:, None] < blk_start[None, :] + L_SEL)
               & (cmp_start[:, None] + L_CMP > blk_start[None, :])).astype(jnp.float32)
    k_blocks = k_slc.reshape(B, n_blk, L_SEL, G, DK).transpose(0, 3, 1, 2, 4)
    v_blocks = v_slc.reshape(B, n_blk, L_SEL, G, DK).transpose(0, 3, 1, 2, 4)
    k_win_p = jnp.pad(k_win, ((0, 0), (WINDOW, 0), (0, 0), (0, 0)))
    v_win_p = jnp.pad(v_win, ((0, 0), (WINDOW, 0), (0, 0), (0, 0)))
    bi = jnp.arange(B)[:, None, None, None]
    gi = jnp.arange(G)[None, :, None, None]
    q_chunks = q.reshape(B, n_q, Q_BLOCK, G, HPG, DK).transpose(1, 0, 2, 3, 4, 5)
    g_chunks = gates.reshape(B, n_q, Q_BLOCK, G, HPG, 3).transpose(1, 0, 2, 3, 4, 5)
    starts = jnp.arange(n_q, dtype=jnp.int32) * Q_BLOCK

    def query_block(args):
        q_c, g_c, s0 = args
        t = s0 + jnp.arange(Q_BLOCK, dtype=jnp.int32)
        s = jnp.einsum('btghd,bngd->bghtn', q_c, k_cmp) * scale
        p_cmp = masked_softmax(s, cmp_end[None, :] <= t[:, None])
        o_cmp = jnp.einsum('bghtn,bngd->btghd', p_cmp.astype(v_cmp.dtype), v_cmp)
        imp = jnp.einsum('bghtn,nj->bgtj', p_cmp, overlap)
        cur = (t // L_SEL)[:, None]
        j = jnp.arange(n_blk)[None, :]
        forced = (j == 0) | (j == cur) | (j == cur - 1)
        score = jnp.where(forced, NEG_BIG, jnp.where(blk_start[None, :] <= t[:, None], imp, -NEG_BIG))
        _, idx = lax.top_k(score, n_sel)
        k_g = k_blocks[bi, gi, idx]
        v_g = v_blocks[bi, gi, idx]
        s = jnp.einsum('btghd,bgtnld->bghtnl', q_c, k_g) * scale
        kpos = idx[..., None] * L_SEL + jnp.arange(L_SEL)
        mask = (kpos <= t[None, None, :, None, None])[:, :, None]
        flat = n_sel * L_SEL
        p = masked_softmax(s.reshape(B, G, HPG, Q_BLOCK, flat),
                           mask.reshape(B, G, 1, Q_BLOCK, flat)).reshape(s.shape)
        o_slc = jnp.einsum('bghtnl,bgtnld->btghd', p.astype(v_g.dtype), v_g)
        k_w = lax.dynamic_slice_in_dim(k_win_p, s0, WINDOW + Q_BLOCK, axis=1)
        v_w = lax.dynamic_slice_in_dim(v_win_p, s0, WINDOW + Q_BLOCK, axis=1)
        kpos_w = s0 - WINDOW + jnp.arange(WINDOW + Q_BLOCK, dtype=jnp.int32)
        diff = t[:, None] - kpos_w[None, :]
        mask_w = (diff >= 0) & (diff < WINDOW) & (kpos_w[None, :] >= 0)
        s = jnp.einsum('btghd,bkgd->bghtk', q_c, k_w) * scale
        p = masked_softmax(s, mask_w)
        o_win = jnp.einsum('bghtk,bkgd->btghd', p.astype(v_w.dtype), v_w)
        o = g_c[..., 0:1] * o_cmp + g_c[..., 1:2] * o_slc + g_c[..., 2:3] * o_win
        return o.reshape(B, Q_BLOCK, G * HPG * DK)

    out = lax.map(query_block, (q_chunks, g_chunks, starts))
    return out.transpose(1, 0, 2, 3).reshape(B, S, G * HPG * DK)


def conformer_conv(a, dw_w, dw_b, ln_g, ln_b, pw_w, pw_b):
    u, v = jnp.split(a, 2, axis=-1)
    u = u * jax.nn.sigmoid(v)
    y = lax.conv_general_dilated(u, dw_w[:, None, :], window_strides=(1,),
                                 padding=[(CONV_WIDTH - 1, 0)],
                                 dimension_numbers=('NWC', 'WIO', 'NWC'),
                                 feature_group_count=CONV_CH)
    y = jax.nn.silu(layer_norm(y + dw_b, ln_g, ln_b))
    return y @ pw_w + pw_b


def peer_ffn(h, w_q, sub_keys, u_emb, v_emb):
    B, S, D = h.shape
    n_tok = B * S
    hf = h.reshape(n_tok, D)
    q = (hf @ w_q).reshape(n_tok, PEER_HEADS, 2, PEER_HALF)
    s = jnp.einsum('nhpd,hpkd->nhpk', q, sub_keys).astype(jnp.float32)
    s1, i1 = lax.top_k(s[:, :, 0], PEER_TOPK)
    s2, i2 = lax.top_k(s[:, :, 1], PEER_TOPK)
    cand_s = (s1[..., :, None] + s2[..., None, :]).reshape(n_tok, PEER_HEADS, PEER_TOPK * PEER_TOPK)
    cand_i = (i1[..., :, None] * PEER_NKEYS + i2[..., None, :]).reshape(n_tok, PEER_HEADS, PEER_TOPK * PEER_TOPK)
    top_s, pos = lax.top_k(cand_s, PEER_TOPK)
    expert = jnp.take_along_axis(cand_i, pos, axis=-1)
    gate = jax.nn.softmax(top_s, axis=-1)
    n_chunk = n_tok // PEER_CHUNK
    E = PEER_HEADS * PEER_TOPK
    xs = (hf.reshape(n_chunk, PEER_CHUNK, D),
          expert.reshape(n_chunk, PEER_CHUNK, E),
          gate.reshape(n_chunk, PEER_CHUNK, E))

    def token_chunk(args):
        x_c, e_c, g_c = args
        a = jnp.einsum('td,ted->te', x_c, u_emb[e_c]).astype(jnp.float32)
        coef = (jax.nn.gelu(a) * g_c).astype(v_emb.dtype)
        return jnp.einsum('te,ted->td', coef, v_emb[e_c])

    return lax.map(token_chunk, xs).reshape(B, S, D)


def hybrid_layer(x, c, norm1_g, norm2_g, w_ada, b_ada, w_in, b_in, q_norm_g, k_norm_g,
                 cmp_pos_k, cmp_pos_v, cmp_k_w1, cmp_k_w2, cmp_v_w1, cmp_v_w2, w_nsa_out,
                 conv_dw_w, conv_dw_b, conv_ln_g, conv_ln_b, conv_pw_w, conv_pw_b, w_out,
                 peer_w_q, peer_sub_keys, peer_u, peer_v):
    B, S, D = x.shape
    mod = jax.nn.silu(c) @ w_ada + b_ada
    shift1, scale1, gate1, shift2, scale2, gate2 = jnp.split(mod[:, None, :], 6, axis=-1)
    h = rms_norm(x, norm1_g) * (1.0 + scale1) + shift1
    z = h @ w_in + b_in
    q, kc, vc, ks, vs, kw, vw, g_nsa, a_glu, g_merge = jnp.split(z, IN_SPLIT_POINTS, axis=-1)
    gsh = (B, S, N_KV_GROUPS, HEAD_DIM)
    q = rms_norm(q.reshape(B, S, N_KV_GROUPS, HEADS_PER_GROUP, HEAD_DIM), q_norm_g)
    k_cmp = rms_norm(compress_blocks(kc.reshape(gsh), cmp_pos_k, cmp_k_w1, cmp_k_w2), k_norm_g[0])
    v_cmp = compress_blocks(vc.reshape(gsh), cmp_pos_v, cmp_v_w1, cmp_v_w2)
    k_slc = rms_norm(ks.reshape(gsh), k_norm_g[1])
    k_win = rms_norm(kw.reshape(gsh), k_norm_g[2])
    gates = jax.nn.sigmoid(g_nsa.reshape(B, S, N_KV_GROUPS, HEADS_PER_GROUP, 3))
    y_attn = nsa_attention(q, k_cmp, v_cmp, k_slc, vs.reshape(gsh), k_win, vw.reshape(gsh), gates) @ w_nsa_out
    y_conv = conformer_conv(a_glu, conv_dw_w, conv_dw_b, conv_ln_g, conv_ln_b, conv_pw_w, conv_pw_b)
    g_m = jax.nn.sigmoid(g_merge.reshape(B, S, 2, D))
    mixed = (g_m[:, :, 0] * y_attn + g_m[:, :, 1] * y_conv) @ w_out
    x = x + gate1 * mixed
    h2 = rms_norm(x, norm2_g) * (1.0 + scale2) + shift2
    return x + gate2 * peer_ffn(h2, peer_w_q, peer_sub_keys, peer_u, peer_v)


def setup_inputs(seed: int = 0) -> dict:
    key = jax.random.key(seed)
    ks = jax.random.split(key, 28)
    f32 = jnp.float32
    L = DEPTH

    def nrm(k, shape, s):
        return jax.random.normal(k, shape, f32) * s

    return {
        'x': nrm(ks[0], (BATCH, SEQ, D_MODEL), 1.0),
        'c': nrm(ks[1], (BATCH, D_MODEL), 1.0),
        'norm1_g': 1.0 + nrm(ks[2], (L, D_MODEL), 0.02),
        'norm2_g': 1.0 + nrm(ks[3], (L, D_MODEL), 0.02),
        'w_ada': nrm(ks[4], (L, D_MODEL, 6 * D_MODEL), 0.5 * D_MODEL ** -0.5),
        'b_ada': nrm(ks[5], (L, 6 * D_MODEL), 0.02),
        'w_in': nrm(ks[6], (L, D_MODEL, IN_COLS), D_MODEL ** -0.5),
        'b_in': nrm(ks[7], (L, IN_COLS), 0.02),
        'q_norm_g': 1.0 + nrm(ks[8], (L, HEAD_DIM), 0.02),
        'k_norm_g': 1.0 + nrm(ks[9], (L, 3, HEAD_DIM), 0.02),
        'cmp_pos_k': nrm(ks[10], (L, L_CMP, HEAD_DIM), 0.1),
        'cmp_pos_v': nrm(ks[11], (L, L_CMP, HEAD_DIM), 0.1),
        'cmp_k_w1': nrm(ks[12], (L, L_CMP * HEAD_DIM, CMP_HIDDEN), (L_CMP * HEAD_DIM) ** -0.5),
        'cmp_k_w2': nrm(ks[13], (L, CMP_HIDDEN, HEAD_DIM), CMP_HIDDEN ** -0.5),
        'cmp_v_w1': nrm(ks[14], (L, L_CMP * HEAD_DIM, CMP_HIDDEN), (L_CMP * HEAD_DIM) ** -0.5),
        'cmp_v_w2': nrm(ks[15], (L, CMP_HIDDEN, HEAD_DIM), CMP_HIDDEN ** -0.5),
        'w_nsa_out': nrm(ks[16], (L, Q_COLS, D_MODEL), Q_COLS ** -0.5),
        'conv_dw_w': nrm(ks[17], (L, CONV_WIDTH, CONV_CH), CONV_WIDTH ** -0.5),
        'conv_dw_b': nrm(ks[18], (L, CONV_CH), 0.02),
        'conv_ln_g': 1.0 + nrm(ks[19], (L, CONV_CH), 0.02),
        'conv_ln_b': nrm(ks[20], (L, CONV_CH), 0.02),
        'conv_pw_w': nrm(ks[21], (L, CONV_CH, D_MODEL), CONV_CH ** -0.5),
        'conv_pw_b': nrm(ks[22], (L, D_MODEL), 0.02),
        'w_out': nrm(ks[23], (L, D_MODEL, D_MODEL), D_MODEL ** -0.5),
        'peer_w_q': nrm(ks[24], (L, D_MODEL, PEER_HEADS * PEER_QDIM), D_MODEL ** -0.5),
        'peer_sub_keys': nrm(ks[25], (L, PEER_HEADS, 2, PEER_NKEYS, PEER_HALF), PEER_HALF ** -0.5),
        'peer_u': nrm(ks[26], (L, PEER_EXPERTS, D_MODEL), D_MODEL ** -0.5),
        'peer_v': nrm(ks[27], (L, PEER_EXPERTS, D_MODEL), 1.0),
    }


def reference(x, c, norm1_g, norm2_g, w_ada, b_ada, w_in, b_in, q_norm_g, k_norm_g,
              cmp_pos_k, cmp_pos_v, cmp_k_w1, cmp_k_w2, cmp_v_w1, cmp_v_w2, w_nsa_out,
              conv_dw_w, conv_dw_b, conv_ln_g, conv_ln_b, conv_pw_w, conv_pw_b, w_out,
              peer_w_q, peer_sub_keys, peer_u, peer_v):
    for l in range(DEPTH):
        x = hybrid_layer(x, c, norm1_g[l], norm2_g[l], w_ada[l], b_ada[l], w_in[l], b_in[l],
                         q_norm_g[l], k_norm_g[l], cmp_pos_k[l], cmp_pos_v[l],
                         cmp_k_w1[l], cmp_k_w2[l], cmp_v_w1[l], cmp_v_w2[l], w_nsa_out[l],
                         conv_dw_w[l], conv_dw_b[l], conv_ln_g[l], conv_ln_b[l],
                         conv_pw_w[l], conv_pw_b[l], w_out[l],
                         peer_w_q[l], peer_sub_keys[l], peer_u[l], peer_v[l])
    return x
```

```python
import functools
import math

import jax
import jax.numpy as jnp
from jax import lax
from jax.experimental import pallas as pl
from jax.experimental.pallas import tpu as pltpu

F32 = jnp.float32
BF16 = jnp.bfloat16
I32 = jnp.int32

N_HEADS = 16
N_KV_GROUPS = 4
HEADS_PER_GROUP = N_HEADS // N_KV_GROUPS
HEAD_DIM = 128
L_CMP = 32
STRIDE_CMP = 16
L_SEL = 64
N_SEL = 16
WINDOW = 512
CONV_WIDTH = 31
PEER_HEADS = 8
PEER_NKEYS = 128
PEER_HALF = 128
PEER_TOPK = 16
EPS = 1e-6
NEG_BIG = 1e30

LANES = 128
SUBLANES = 8
VMEM_LIMIT = 48 * 1024 * 1024

MM_TM = 1024
MM_TN = 512
ROW_TILE = 256
NSA_TQ = 128
NSA_TK = 128
CONV_TS = 256
CONV_HALO = 32
PEER_TM = 256
PEER_TB = 8

Q_COLS = N_HEADS * HEAD_DIM
KV_COLS = N_KV_GROUPS * HEAD_DIM
GATE_PAD = N_KV_GROUPS * LANES


def _cparams(sem):
    return pltpu.CompilerParams(dimension_semantics=sem, vmem_limit_bytes=VMEM_LIMIT)


def _gelu_tanh(x):
    c = math.sqrt(2.0 / math.pi)
    return x * (0.5 * (1.0 + jnp.tanh(c * (x + 0.044715 * (x * x * x)))))


def _sigmoid(x):
    return 1.0 / (1.0 + jnp.exp(-x))


def _mm_bias_body(a_ref, b_ref, bias_ref, o_ref):
    acc = jnp.dot(a_ref[...].astype(BF16), b_ref[...].astype(BF16), preferred_element_type=F32)
    o_ref[...] = (acc + bias_ref[...]).astype(o_ref.dtype)


def _matmul_bias(a, b, bias, out_dtype=F32, tm=MM_TM, tn=MM_TN):
    m, k = a.shape
    n = b.shape[1]
    tm = min(tm, m)
    tn = min(tn, n)
    assert m % tm == 0 and n % tn == 0
    return pl.pallas_call(
        _mm_bias_body,
        grid=(m // tm, n // tn),
        in_specs=[
            pl.BlockSpec((tm, k), lambda i, j: (i, 0)),
            pl.BlockSpec((k, tn), lambda i, j: (0, j)),
            pl.BlockSpec((1, tn), lambda i, j: (0, j)),
        ],
        out_specs=pl.BlockSpec((tm, tn), lambda i, j: (i, j)),
        out_shape=jax.ShapeDtypeStruct((m, n), out_dtype),
        compiler_params=_cparams(("parallel", "arbitrary")),
        name="mm_bias",
    )(a, b, bias)


def _mm_merge_body(a_ref, b_ref, bias_ref, g0_ref, g1_ref, ya_ref, o_ref):
    acc = jnp.dot(a_ref[...], b_ref[...].astype(BF16), preferred_element_type=F32) + bias_ref[...]
    o_ref[...] = (_sigmoid(g0_ref[...]) * ya_ref[...] + _sigmoid(g1_ref[...]) * acc).astype(o_ref.dtype)


def _matmul_merge(a, b, bias, z, g0_col, g1_col, y_attn, tm=MM_TM, tn=MM_TN):
    m, k = a.shape
    n = b.shape[1]
    assert m % tm == 0 and n % tn == 0 and g0_col % tn == 0 and g1_col % tn == 0
    return pl.pallas_call(
        _mm_merge_body,
        grid=(m // tm, n // tn),
        in_specs=[
            pl.BlockSpec((tm, k), lambda i, j: (i, 0)),
            pl.BlockSpec((k, tn), lambda i, j: (0, j)),
            pl.BlockSpec((1, tn), lambda i, j: (0, j)),
            pl.BlockSpec((tm, tn), lambda i, j: (i, g0_col // tn + j)),
            pl.BlockSpec((tm, tn), lambda i, j: (i, g1_col // tn + j)),
            pl.BlockSpec((tm, tn), lambda i, j: (i, j)),
        ],
        out_specs=pl.BlockSpec((tm, tn), lambda i, j: (i, j)),
        out_shape=jax.ShapeDtypeStruct((m, n), BF16),
        compiler_params=_cparams(("parallel", "arbitrary")),
        name="mm_merge",
    )(a, b, bias, z, z, y_attn)


def _mm_resid_body(a_ref, b_ref, res_ref, gate_ref, o_ref):
    acc = jnp.dot(a_ref[...], b_ref[...].astype(BF16), preferred_element_type=F32)
    o_ref[...] = res_ref[...] + gate_ref[...] * acc


def _matmul_resid(a, b, res, mod6, gate_idx, seq, tm=MM_TM, tn=MM_TN):
    m, k = a.shape
    n = b.shape[1]
    assert m % tm == 0 and n % tn == 0 and seq % tm == 0
    return pl.pallas_call(
        _mm_resid_body,
        grid=(m // tm, n // tn),
        in_specs=[
            pl.BlockSpec((tm, k), lambda i, j: (i, 0)),
            pl.BlockSpec((k, tn), lambda i, j: (0, j)),
            pl.BlockSpec((tm, tn), lambda i, j: (i, j)),
            pl.BlockSpec((None, 1, tn), lambda i, j: ((i * tm // seq) * 6 + gate_idx, 0, j)),
        ],
        out_specs=pl.BlockSpec((tm, tn), lambda i, j: (i, j)),
        out_shape=jax.ShapeDtypeStruct((m, n), F32),
        compiler_params=_cparams(("parallel", "arbitrary")),
        name="mm_resid",
    )(a, b, res, mod6)


def _norm_mod_body(x_ref, g_ref, shift_ref, scale_ref, hb_ref, *maybe_hf_ref):
    x = x_ref[...]
    y = x * lax.rsqrt(jnp.mean(x * x, axis=-1, keepdims=True) + EPS)
    h = (y * g_ref[...]) * (1.0 + scale_ref[...]) + shift_ref[...]
    hb_ref[...] = h.astype(BF16)
    if maybe_hf_ref:
        maybe_hf_ref[0][...] = h


def _norm_mod(x2d, g, mod6, shift_idx, scale_idx, seq, want_f32, tr=ROW_TILE):
    n, d = x2d.shape
    assert n % tr == 0 and seq % tr == 0
    out_shape = [jax.ShapeDtypeStruct((n, d), BF16)]
    out_specs = [pl.BlockSpec((tr, d), lambda i: (i, 0))]
    if want_f32:
        out_shape.append(jax.ShapeDtypeStruct((n, d), F32))
        out_specs.append(pl.BlockSpec((tr, d), lambda i: (i, 0)))
    return pl.pallas_call(
        _norm_mod_body,
        grid=(n // tr,),
        in_specs=[
            pl.BlockSpec((tr, d), lambda i: (i, 0)),
            pl.BlockSpec((1, d), lambda i: (0, 0)),
            pl.BlockSpec((None, 1, d), lambda i: ((i * tr // seq) * 6 + shift_idx, 0, 0)),
            pl.BlockSpec((None, 1, d), lambda i: ((i * tr // seq) * 6 + scale_idx, 0, 0)),
        ],
        out_specs=out_specs,
        out_shape=out_shape,
        compiler_params=_cparams(("parallel",)),
        name="norm_mod",
    )(x2d, g, mod6, mod6)


def _head_norm_body(q_ref, sl_ref, wn_ref, gq_ref, gs_ref, gw_ref, qn_ref, kv_ref):
    def rms(x, g):
        y = x * lax.rsqrt(jnp.mean(x * x, axis=-1, keepdims=True) + EPS)
        return y * g

    qscale = HEAD_DIM ** -0.5
    for c in range(N_HEADS):
        sl = slice(c * HEAD_DIM, (c + 1) * HEAD_DIM)
        qn_ref[:, sl] = (rms(q_ref[:, sl], gq_ref[...]) * qscale).astype(BF16)
    for g in range(N_KV_GROUPS):
        sl = slice(g * HEAD_DIM, (g + 1) * HEAD_DIM)
        sv = slice(KV_COLS + g * HEAD_DIM, KV_COLS + (g + 1) * HEAD_DIM)
        kv_ref[:, sl] = rms(sl_ref[:, sl], gs_ref[...]).astype(BF16)
        kv_ref[:, sv] = sl_ref[:, sv].astype(BF16)
        kv_ref[:, slice(2 * KV_COLS + g * HEAD_DIM, 2 * KV_COLS + (g + 1) * HEAD_DIM)] = (
            rms(wn_ref[:, sl], gw_ref[...]).astype(BF16))
        kv_ref[:, slice(3 * KV_COLS + g * HEAD_DIM, 3 * KV_COLS + (g + 1) * HEAD_DIM)] = (
            wn_ref[:, sv].astype(BF16))


def _head_norm(z, col_q, col_slc, col_win, gq, gs, gw, tr=ROW_TILE):
    n = z.shape[0]
    w2 = 2 * KV_COLS
    assert n % tr == 0 and col_q % Q_COLS == 0 and col_slc % w2 == 0 and col_win % w2 == 0
    return pl.pallas_call(
        _head_norm_body,
        grid=(n // tr,),
        in_specs=[
            pl.BlockSpec((tr, Q_COLS), lambda i: (i, col_q // Q_COLS)),
            pl.BlockSpec((tr, w2), lambda i: (i, col_slc // w2)),
            pl.BlockSpec((tr, w2), lambda i: (i, col_win // w2)),
            pl.BlockSpec((1, HEAD_DIM), lambda i: (0, 0)),
            pl.BlockSpec((1, HEAD_DIM), lambda i: (0, 0)),
            pl.BlockSpec((1, HEAD_DIM), lambda i: (0, 0)),
        ],
        out_specs=[
            pl.BlockSpec((tr, Q_COLS), lambda i: (i, 0)),
            pl.BlockSpec((tr, 4 * KV_COLS), lambda i: (i, 0)),
        ],
        out_shape=[
            jax.ShapeDtypeStruct((n, Q_COLS), BF16),
            jax.ShapeDtypeStruct((n, 4 * KV_COLS), BF16),
        ],
        compiler_params=_cparams(("parallel",)),
        name="head_norm",
    )(z, z, z, gq, gs, gw)


def _compress_body(c_ref, pa_ref, pb_ref, wt_ref, wb_ref, w2_ref, g_ref, o_ref, *, normalize):
    c = c_ref[...]
    top = jnp.dot((c + pa_ref[...]).astype(BF16), wt_ref[...].astype(BF16), preferred_element_type=F32)
    bot = jnp.dot((c + pb_ref[...]).astype(BF16), wb_ref[...].astype(BF16), preferred_element_type=F32)
    n_chunk = c.shape[0]
    pre = top + pltpu.roll(bot, n_chunk - 1, 0)
    out = jnp.dot(_gelu_tanh(pre).astype(BF16), w2_ref[...].astype(BF16), preferred_element_type=F32)
    if normalize:
        out = out * lax.rsqrt(jnp.mean(out * out, axis=-1, keepdims=True) + EPS) * g_ref[...]
    o_ref[...] = out.astype(BF16)


def _compress(chunks, pos, w1, w2, gain, normalize):
    b, g, n_chunk, ck = chunks.shape
    half = STRIDE_CMP * HEAD_DIM
    pa = pos[:STRIDE_CMP].reshape(1, half)
    pb = pos[STRIDE_CMP:].reshape(1, half)
    full = lambda *shape: pl.BlockSpec(shape, lambda bi, gi: (0,) * len(shape))
    return pl.pallas_call(
        functools.partial(_compress_body, normalize=normalize),
        grid=(b, g),
        in_specs=[
            pl.BlockSpec((None, None, n_chunk, ck), lambda bi, gi: (bi, gi, 0, 0)),
            full(1, half), full(1, half),
            full(half, HEAD_DIM), full(half, HEAD_DIM), full(HEAD_DIM, HEAD_DIM), full(1, HEAD_DIM),
        ],
        out_specs=pl.BlockSpec((None, None, n_chunk, HEAD_DIM), lambda bi, gi: (bi, gi, 0, 0)),
        out_shape=jax.ShapeDtypeStruct((b, g, n_chunk, HEAD_DIM), BF16),
        compiler_params=_cparams(("parallel", "parallel")),
        name="compress",
    )(chunks, pa, pb, w1[:half], w1[half:], w2, gain)


def _nsa_body(q_ref, kc_ref, vc_ref, ks_ref, vs_ref, kw_ref, vw_ref, g_ref, o_ref,
              ms_ref, ls_ref, as_ref, mw_ref, lw_ref, aw_ref, *, tq, tk, seq):
    n_cmp = (seq - L_CMP) // STRIDE_CMP + 1
    n_blk = seq // L_SEL
    n_sel = min(N_SEL, n_blk)
    ncp = kc_ref.shape[0]
    hpg = HEADS_PER_GROUP
    nt_dims = (((1,), (1,)), ((), ()))

    t0 = pl.program_id(2) * tq
    t_col = t0 + lax.broadcasted_iota(I32, (tq, 1), 0)

    n_row = lax.broadcasted_iota(I32, (1, ncp), 1)
    vis = (n_row * STRIDE_CMP + (L_CMP - 1) <= t_col) & (n_row < n_cmp)
    kc = kc_ref[...]
    vc = vc_ref[...]
    psum = jnp.zeros((tq, ncp), F32)
    o_cmp = []
    for h in range(hpg):
        qh = q_ref[:, h * HEAD_DIM:(h + 1) * HEAD_DIM]
        s = lax.dot_general(qh, kc, nt_dims, preferred_element_type=F32)
        s = jnp.where(vis, s, -NEG_BIG)
        m = jnp.max(s, axis=-1, keepdims=True)
        e = jnp.where(vis, jnp.exp(s - m), 0.0)
        p = e / jnp.maximum(jnp.sum(e, axis=-1, keepdims=True), 1e-30)
        psum = psum + p
        o_cmp.append(jnp.dot(p.astype(BF16), vc, preferred_element_type=F32))

    j_row = lax.broadcasted_iota(I32, (1, LANES), 1)
    n_colv = lax.broadcasted_iota(I32, (ncp, 1), 0)
    overlap = ((n_colv * STRIDE_CMP < j_row * L_SEL + L_SEL)
               & (n_colv * STRIDE_CMP + L_CMP > j_row * L_SEL)
               & (n_colv < n_cmp) & (j_row < n_blk))
    overlap = jnp.where(overlap, 1.0, 0.0).astype(BF16)
    p_hi = psum.astype(BF16)
    r1 = psum - p_hi.astype(F32)
    p_mid = r1.astype(BF16)
    p_lo = (r1 - p_mid.astype(F32)).astype(BF16)
    imp = (jnp.dot(p_hi, overlap, preferred_element_type=F32)
           + jnp.dot(p_mid, overlap, preferred_element_type=F32)
           + jnp.dot(p_lo, overlap, preferred_element_type=F32))

    sel_shift = L_SEL.bit_length() - 1
    cur = jnp.right_shift(t_col, sel_shift)
    forced = (j_row == 0) | (j_row == cur) | (j_row == cur - 1)
    score = jnp.where(forced, NEG_BIG, jnp.where(j_row * L_SEL <= t_col, imp, -NEG_BIG))
    score = jnp.where(j_row < n_blk, score, -3e38)
    rank = jnp.zeros((tq, LANES), F32)
    for i in range(n_blk):
        si = score[:, i:i + 1]
        ge = jnp.where(si >= score, 1.0, 0.0)
        gt = jnp.where(si > score, 1.0, 0.0)
        rank = rank + jnp.where(j_row > i, ge, gt)
    sel = jnp.where(rank < n_sel, 1.0, 0.0).astype(BF16)

    c_row = lax.broadcasted_iota(I32, (1, tk), 1)
    j_colv = lax.broadcasted_iota(I32, (LANES, 1), 0)

    def slc_mask(k0):
        kpos = k0 + c_row
        expand = jnp.where(j_colv == jnp.right_shift(kpos, sel_shift), 1.0, 0.0).astype(BF16)
        chosen = jnp.dot(sel, expand, preferred_element_type=F32) > 0.5
        return chosen & (kpos <= t_col)

    def win_mask(k0):
        d = t_col - (k0 + c_row)
        return (d >= 0) & (d < WINDOW)

    def flash(k_ref, v_ref, kb_lo, kb_hi, mask_fn, m_ref, l_ref, a_ref):
        m_ref[...] = jnp.full(m_ref.shape, -NEG_BIG, F32)
        l_ref[...] = jnp.zeros(l_ref.shape, F32)
        a_ref[...] = jnp.zeros(a_ref.shape, F32)

        def body(kb, carry):
            k0 = pl.multiple_of(kb * tk, tk)
            kblk = k_ref[pl.ds(k0, tk), :]
            vblk = v_ref[pl.ds(k0, tk), :]
            mask = mask_fn(k0)
            for h in range(hpg):
                qh = q_ref[:, h * HEAD_DIM:(h + 1) * HEAD_DIM]
                s = lax.dot_general(qh, kblk, nt_dims, preferred_element_type=F32)
                s = jnp.where(mask, s, -NEG_BIG)
                m_old = m_ref[h]
                m_new = jnp.maximum(m_old, jnp.max(s, axis=-1, keepdims=True))
                p = jnp.where(mask, jnp.exp(s - m_new), 0.0)
                alpha = jnp.exp(m_old - m_new)
                l_ref[h] = alpha * l_ref[h] + jnp.sum(p, axis=-1, keepdims=True)
                a_ref[h] = alpha * a_ref[h] + jnp.dot(p.astype(BF16), vblk, preferred_element_type=F32)
                m_ref[h] = m_new
            return carry

        lax.fori_loop(kb_lo, kb_hi, body, 0)

    kb_hi = lax.div(t0 + (tq + tk - 1), tk)
    flash(ks_ref, vs_ref, 0, kb_hi, slc_mask, ms_ref, ls_ref, as_ref)
    kb_lo = lax.div(jnp.maximum(t0 - (WINDOW - 1), 0), tk)
    flash(kw_ref, vw_ref, kb_lo, kb_hi, win_mask, mw_ref, lw_ref, aw_ref)

    gates = _sigmoid(g_ref[...])
    for h in range(hpg):
        o = (gates[:, 3 * h:3 * h + 1] * o_cmp[h]
             + gates[:, 3 * h + 1:3 * h + 2] * (as_ref[h] / ls_ref[h])
             + gates[:, 3 * h + 2:3 * h + 3] * (aw_ref[h] / lw_ref[h]))
        o_ref[:, h * HEAD_DIM:(h + 1) * HEAD_DIM] = o.astype(BF16)


def _nsa(qn, kvn, kcn, vcm, z, col_gate, batch, seq, tq=NSA_TQ, tk=NSA_TK):
    n = qn.shape[0]
    g = N_KV_GROUPS
    gw = HEADS_PER_GROUP * HEAD_DIM
    nq = seq // tq
    ncp = kcn.shape[2]
    assert seq % tq == 0 and seq % tk == 0 and seq // L_SEL <= LANES and col_gate % LANES == 0
    kv_spec = lambda part: pl.BlockSpec((seq, HEAD_DIM), lambda b, gi, i: (b, part * g + gi))
    cmp_spec = pl.BlockSpec((None, None, ncp, HEAD_DIM), lambda b, gi, i: (b, gi, 0, 0))
    stat = pltpu.VMEM((HEADS_PER_GROUP, tq, 1), F32)
    accs = pltpu.VMEM((HEADS_PER_GROUP, tq, HEAD_DIM), F32)
    return pl.pallas_call(
        functools.partial(_nsa_body, tq=tq, tk=tk, seq=seq),
        grid=(batch, g, nq),
        in_specs=[
            pl.BlockSpec((tq, gw), lambda b, gi, i: (b * nq + i, gi)),
            cmp_spec, cmp_spec,
            kv_spec(0), kv_spec(1), kv_spec(2), kv_spec(3),
            pl.BlockSpec((tq, LANES), lambda b, gi, i: (b * nq + i, col_gate // LANES + gi)),
        ],
        out_specs=pl.BlockSpec((tq, gw), lambda b, gi, i: (b * nq + i, gi)),
        out_shape=jax.ShapeDtypeStruct((n, Q_COLS), BF16),
        scratch_shapes=[stat, stat, accs, stat, stat, accs],
        compiler_params=_cparams(("parallel", "parallel", "arbitrary")),
        name="nsa",
    )(qn, kcn, vcm, kvn, kvn, kvn, kvn, z)


def _conv_body(u_ref, v_ref, hu_ref, hv_ref, w_ref, b_ref, lg_ref, lb_ref, o_ref, ubuf, ybuf, *, ts, halo):
    i = pl.program_id(1)
    hal = hu_ref[...] * _sigmoid(hv_ref[...])
    ubuf[0:halo, :] = jnp.where(i > 0, hal, 0.0)
    ubuf[halo:halo + ts, :] = u_ref[...] * _sigmoid(v_ref[...])
    ch = u_ref.shape[1]
    off = halo - (CONV_WIDTH - 1)
    for c in range(ch // LANES):
        sl = slice(c * LANES, (c + 1) * LANES)
        acc = jnp.zeros((ts, LANES), F32)
        for k in range(CONV_WIDTH):
            acc = acc + ubuf[off + k:off + k + ts, sl] * w_ref[k:k + 1, sl]
        ybuf[:, sl] = acc + b_ref[:, sl]
    y = ybuf[...]
    mu = jnp.mean(y, axis=-1, keepdims=True)
    yc = y - mu
    yn = yc * lax.rsqrt(jnp.mean(yc * yc, axis=-1, keepdims=True) + EPS) * lg_ref[...] + lb_ref[...]
    o_ref[...] = (yn * _sigmoid(yn)).astype(BF16)


def _conv_module(z, col_u, col_v, dw_w, dw_b, ln_g, ln_b, batch, seq, ts=CONV_TS, halo=CONV_HALO):
    n = z.shape[0]
    ch = dw_w.shape[1]
    ns = seq // ts
    r = ts // halo
    assert seq % ts == 0 and ts % halo == 0 and col_u % ch == 0 and col_v % ch == 0
    halo_map = lambda col: (lambda b, i: (jnp.maximum((b * ns + i) * r - 1, 0), col // ch))
    full = lambda *shape: pl.BlockSpec(shape, lambda b, i: (0,) * len(shape))
    return pl.pallas_call(
        functools.partial(_conv_body, ts=ts, halo=halo),
        grid=(batch, ns),
        in_specs=[
            pl.BlockSpec((ts, ch), lambda b, i: (b * ns + i, col_u // ch)),
            pl.BlockSpec((ts, ch), lambda b, i: (b * ns + i, col_v // ch)),
            pl.BlockSpec((halo, ch), halo_map(col_u)),
            pl.BlockSpec((halo, ch), halo_map(col_v)),
            full(CONV_WIDTH, ch), full(1, ch), full(1, ch), full(1, ch),
        ],
        out_specs=pl.BlockSpec((ts, ch), lambda b, i: (b * ns + i, 0)),
        out_shape=jax.ShapeDtypeStruct((n, ch), BF16),
        scratch_shapes=[pltpu.VMEM((ts + halo, ch), F32), pltpu.VMEM((ts, ch), F32)],
        compiler_params=_cparams(("parallel", "arbitrary")),
        name="conv_module",
    )(z, z, z, z, dw_w, dw_b, ln_g, ln_b)


def _peer_topk_body(q_ref, sk_ref, e_ref, g_ref, work, v1, i1, v2, i2, cand, candi, ts_buf, te_buf):
    k = PEER_TOPK
    tmn = q_ref.shape[0]
    nt_dims = (((1,), (1,)), ((), ()))

    def extract(src, nrows, val_out, idx_out, payload=None):
        rows = lax.broadcasted_iota(I32, (nrows, tmn), 0).astype(F32)

        def body(r, carry):
            x = src[0:nrows, :]
            m = jnp.max(x, axis=0, keepdims=True)
            pos = jnp.min(jnp.where(x == m, rows, float(nrows)), axis=0, keepdims=True)
            hit = rows == pos
            val_out[pl.ds(r, 1), :] = m
            if payload is None:
                idx_out[pl.ds(r, 1), :] = pos
            else:
                idx_out[pl.ds(r, 1), :] = jnp.max(jnp.where(hit, payload[...], -1.0), axis=0, keepdims=True)
            src[0:nrows, :] = jnp.where(hit, -jnp.inf, x)
            return carry

        lax.fori_loop(0, k, body, 0)

    for p, (vo, io) in enumerate(((v1, i1), (v2, i2))):
        qh = q_ref[:, p * PEER_HALF:(p + 1) * PEER_HALF].astype(BF16)
        work[0:PEER_NKEYS, :] = lax.dot_general(sk_ref[p].astype(BF16), qh, nt_dims,
                                                preferred_element_type=F32)
        extract(work, PEER_NKEYS, vo, io)

    for a in range(k):
        cand[a * k:(a + 1) * k, :] = v1[a:a + 1, :] + v2[...]
        candi[a * k:(a + 1) * k, :] = i1[a:a + 1, :] * float(PEER_NKEYS) + i2[...]
    extract(cand, k * k, ts_buf, te_buf, payload=candi)

    ts = ts_buf[...]
    ex = jnp.exp(ts - jnp.max(ts, axis=0, keepdims=True))
    g_ref[...] = ex / jnp.sum(ex, axis=0, keepdims=True)
    e_ref[...] = te_buf[...].astype(I32)


def _peer_topk(qp, sub_keys, tmn=PEER_TM):
    n = qp.shape[0]
    k = PEER_TOPK
    qd = 2 * PEER_HALF
    sk = sub_keys.reshape(PEER_HEADS, 2, PEER_NKEYS, PEER_HALF)
    assert n % tmn == 0
    rows = PEER_HEADS * k
    small_f = pltpu.VMEM((k, tmn), F32)
    return pl.pallas_call(
        _peer_topk_body,
        grid=(n // tmn, PEER_HEADS),
        in_specs=[
            pl.BlockSpec((tmn, qd), lambda i, h: (i, h)),
            pl.BlockSpec((None, 2, PEER_NKEYS, PEER_HALF), lambda i, h: (h, 0, 0, 0)),
        ],
        out_specs=[
            pl.BlockSpec((k, tmn), lambda i, h: (h, i)),
            pl.BlockSpec((k, tmn), lambda i, h: (h, i)),
        ],
        out_shape=[
            jax.ShapeDtypeStruct((rows, n), I32),
            jax.ShapeDtypeStruct((rows, n), F32),
        ],
        scratch_shapes=[
            pltpu.VMEM((PEER_NKEYS, tmn), F32),
            small_f, small_f, small_f, small_f,
            pltpu.VMEM((k * k, tmn), F32), pltpu.VMEM((k * k, tmn), F32),
            small_f, small_f,
        ],
        compiler_params=_cparams(("parallel", "arbitrary")),
        name="peer_topk",
    )(qp, sk)


def _peer_mix_body(idx_cur, idx_nxt, uv_hbm, h_ref, gate_ref, x_ref, g2_ref, o_ref, buf, cbuf, sem, *, tb, ne):
    i = pl.program_id(0)
    nb = pl.num_programs(0)
    slot = i % 2
    half = h_ref.shape[1]
    rows_per_block = tb * ne

    def row_copy(e_idx, slot_, row):
        return pltpu.make_async_copy(uv_hbm.at[e_idx], buf.at[slot_, row], sem.at[slot_])

    def block_wait(slot_):
        pltpu.make_async_copy(uv_hbm.at[pl.ds(0, rows_per_block)], buf.at[slot_], sem.at[slot_]).wait()

    @pl.when(i == 0)
    def _():
        def issue_first(t, carry):
            for e in range(ne):
                row_copy(idx_cur[t, e], 0, t * ne + e).start()
            return carry
        lax.fori_loop(0, tb, issue_first, 0)

    block_wait(slot)

    lane = lax.broadcasted_iota(I32, (SUBLANES, LANES), 1)

    def token(t, carry):
        for e in range(ne):
            row_copy(idx_nxt[t, e], 1 - slot, t * ne + e).start()
        h = h_ref[t]
        acc = jnp.zeros((SUBLANES, LANES), F32)
        for e in range(ne):
            prod = buf[slot, t * ne + e, 0:half, :].astype(F32) * h
            part = prod[0:SUBLANES]
            for s in range(1, half // SUBLANES):
                part = part + prod[s * SUBLANES:(s + 1) * SUBLANES]
            acc = jnp.where(lane == e, jnp.sum(part, axis=-1, keepdims=True), acc)
        a = jnp.sum(acc, axis=0, keepdims=True)
        coef = _gelu_tanh(a) * gate_ref[pl.ds(t, 1), :]
        cbuf[...] = jnp.transpose(jnp.broadcast_to(coef, (ne, ne)))
        out = jnp.zeros((half, LANES), F32)
        for e in range(ne):
            out = out + cbuf[e:e + 1, :] * buf[slot, t * ne + e, half:2 * half, :].astype(F32)
        o_ref[t] = x_ref[t] + g2_ref[...] * out
        return carry

    lax.fori_loop(0, tb, token, 0)

    @pl.when(i == nb - 1)
    def _():
        block_wait(1 - slot)


def _peer_mix(idx, gate, uv, h2, x1, mod6, gate_idx, seq, tb=PEER_TB):
    n, ne = idx.shape
    d = h2.shape[1]
    half = d // LANES
    nb = n // tb
    assert n % tb == 0 and seq % tb == 0 and ne == LANES
    h3 = h2.reshape(n, half, LANES)
    x3 = x1.reshape(n, half, LANES)
    m3 = mod6.reshape(mod6.shape[0], half, LANES)
    out = pl.pallas_call(
        functools.partial(_peer_mix_body, tb=tb, ne=ne),
        grid=(nb,),
        in_specs=[
            pl.BlockSpec((tb, ne), lambda i: (i, 0), memory_space=pltpu.SMEM),
            pl.BlockSpec((tb, ne), lambda i: (jnp.minimum(i + 1, nb - 1), 0), memory_space=pltpu.SMEM),
            pl.BlockSpec(memory_space=pl.ANY),
            pl.BlockSpec((tb, half, LANES), lambda i: (i, 0, 0)),
            pl.BlockSpec((tb, ne), lambda i: (i, 0)),
            pl.BlockSpec((tb, half, LANES), lambda i: (i, 0, 0)),
            pl.BlockSpec((None, half, LANES), lambda i: ((i * tb // seq) * 6 + gate_idx, 0, 0)),
        ],
        out_specs=pl.BlockSpec((tb, half, LANES), lambda i: (i, 0, 0)),
        out_shape=jax.ShapeDtypeStruct((n, half, LANES), F32),
        scratch_shapes=[
            pltpu.VMEM((2, tb * ne, 2 * half, LANES), BF16),
            pltpu.VMEM((ne, ne), F32),
            pltpu.SemaphoreType.DMA((2,)),
        ],
        compiler_params=_cparams(("arbitrary",)),
        name="peer_mix",
    )(idx, idx, uv, h3, gate, x3, m3)
    return out.reshape(n, d)


def _layer(x, c, norm1_g, norm2_g, w_ada, b_ada, w_in, b_in, q_norm_g, k_norm_g,
           cmp_pos_k, cmp_pos_v, cmp_k_w1, cmp_k_w2, cmp_v_w1, cmp_v_w2, w_nsa_out,
           conv_dw_w, conv_dw_b, conv_ln_g, conv_ln_b, conv_pw_w, conv_pw_b, w_out,
           peer_w_q, peer_sub_keys, peer_u, peer_v):
    b, s, d = x.shape
    n = b * s
    g = N_KV_GROUPS
    conv_ch = conv_dw_w.shape[1]
    x2 = x.reshape(n, d)

    rows = max(SUBLANES, b)
    sc = jnp.zeros((rows, d), F32).at[:b].set(c * _sigmoid(c))
    mod = _matmul_bias(sc, w_ada, b_ada.reshape(1, -1), tm=rows)[:b]
    mod6 = mod.reshape(b * 6, 1, d)

    sizes = (Q_COLS,) + (KV_COLS,) * 6 + (3 * N_HEADS, 2 * conv_ch, 2 * d)
    offs = [0]
    for sz in sizes:
        offs.append(offs[-1] + sz)
    part = lambda arr, k: arr[..., offs[k]:offs[k + 1]]
    per_group = 3 * HEADS_PER_GROUP

    def gate_cols(arr):
        a = part(arr, 7).reshape(arr.shape[:-1] + (g, per_group))
        a = jnp.pad(a, [(0, 0)] * (a.ndim - 1) + [(0, LANES - per_group)])
        return a.reshape(arr.shape[:-1] + (GATE_PAD,))

    order = [0, 8, 9, 1, 2, 3, 4, 5, 6]
    w_cat = jnp.concatenate([part(w_in, k) for k in order] + [gate_cols(w_in)], axis=-1).astype(BF16)
    b_cat = jnp.concatenate([part(b_in, k) for k in order] + [gate_cols(b_in)], axis=-1).reshape(1, -1)
    col_q = 0
    col_glu = Q_COLS
    col_merge = col_glu + 2 * conv_ch
    col_cmp = col_merge + 2 * d
    col_slc = col_cmp + 2 * KV_COLS
    col_win = col_slc + 2 * KV_COLS
    col_gate = col_win + 2 * KV_COLS

    h1, = _norm_mod(x2, norm1_g.reshape(1, d), mod6, 0, 1, s, want_f32=False)
    z = _matmul_bias(h1, w_cat, b_cat)

    qn, kvn = _head_norm(z, col_q, col_slc, col_win, q_norm_g.reshape(1, -1),
                         k_norm_g[1].reshape(1, -1), k_norm_g[2].reshape(1, -1))
    n_chunk = s // STRIDE_CMP

    def chunks(col):
        a = z[:, col:col + KV_COLS].reshape(b, n_chunk, STRIDE_CMP, g, HEAD_DIM)
        return a.transpose(0, 3, 1, 2, 4).reshape(b, g, n_chunk, STRIDE_CMP * HEAD_DIM)

    kcn = _compress(chunks(col_cmp), cmp_pos_k, cmp_k_w1, cmp_k_w2, k_norm_g[0].reshape(1, -1), True)
    vcm = _compress(chunks(col_cmp + KV_COLS), cmp_pos_v, cmp_v_w1, cmp_v_w2, k_norm_g[0].reshape(1, -1), False)

    attn = _nsa(qn, kvn, kcn, vcm, z, col_gate, b, s)
    y_attn = _matmul_bias(attn, w_nsa_out, jnp.zeros((1, d), F32))

    yc = _conv_module(z, col_glu, col_glu + conv_ch, conv_dw_w, conv_dw_b.reshape(1, -1),
                      conv_ln_g.reshape(1, -1), conv_ln_b.reshape(1, -1), b, s)
    merged = _matmul_merge(yc, conv_pw_w, conv_pw_b.reshape(1, -1), z, col_merge, col_merge + d, y_attn)
    x1 = _matmul_resid(merged, w_out, x2, mod6, 2, s)

    h2b, h2f = _norm_mod(x1, norm2_g.reshape(1, d), mod6, 3, 4, s, want_f32=True)
    qp = _matmul_bias(h2b, peer_w_q, jnp.zeros((1, peer_w_q.shape[1]), F32))
    e_t, g_t = _peer_topk(qp, peer_sub_keys)
    idx = e_t.T
    gate = g_t.T
    half = d // LANES
    uv = jnp.concatenate([peer_u.astype(BF16).reshape(-1, half, LANES),
                          peer_v.astype(BF16).reshape(-1, half, LANES)], axis=1)
    out = _peer_mix(idx, gate, uv, h2f, x1, mod6, 5, s)
    return out.reshape(b, s, d)


def kernel(x, c, norm1_g, norm2_g, w_ada, b_ada, w_in, b_in, q_norm_g, k_norm_g, cmp_pos_k, cmp_pos_v,
           cmp_k_w1, cmp_k_w2, cmp_v_w1, cmp_v_w2, w_nsa_out, conv_dw_w, conv_dw_b, conv_ln_g, conv_ln_b,
           conv_pw_w, conv_pw_b, w_out, peer_w_q, peer_sub_keys, peer_u, peer_v):
    depth = norm1_g.shape[0]
    for l in range(depth):
        x = _layer(x, c, norm1_g[l], norm2_g[l], w_ada[l], b_ada[l], w_in[l], b_in[l],
                   q_norm_g[l], k_norm_g[l], cmp_pos_k[l], cmp_pos_v[l],
                   cmp_k_w1[l], cmp_k_w2[l], cmp_v_w1[l], cmp_v_w2[l], w_nsa_out[l],
                   conv_dw_w[l], conv_dw_b[l], conv_ln_g[l], conv_ln_b[l],
                   conv_pw_w[l], conv_pw_b[l], w_out[l],
                   peer_w_q[l], peer_sub_keys[l], peer_u[l], peer_v[l])
    return x
```

```python
import functools
import math

import jax
import jax.numpy as jnp
from jax import lax
from jax.experimental import pallas as pl
from jax.experimental.pallas import tpu as pltpu

F32 = jnp.float32
BF16 = jnp.bfloat16
I32 = jnp.int32

N_HEADS = 16
N_KV_GROUPS = 4
HEADS_PER_GROUP = N_HEADS // N_KV_GROUPS
HEAD_DIM = 128
L_CMP = 32
STRIDE_CMP = 16
L_SEL = 64
N_SEL = 16
WINDOW = 512
CONV_WIDTH = 31
PEER_HEADS = 8
PEER_NKEYS = 128
PEER_HALF = 128
PEER_TOPK = 16
EPS = 1e-6
NEG_BIG = 1e30

LANES = 128
SUBLANES = 8
VMEM_LIMIT = 48 * 1024 * 1024

MM_TM = 1024
MM_TN = 512
ROW_TILE = 256
NSA_TQ = 128
NSA_TK = 512
CONV_TS = 256
CONV_HALO = 32
PEER_TM = 256
PEER_TB = 8

Q_COLS = N_HEADS * HEAD_DIM
KV_COLS = N_KV_GROUPS * HEAD_DIM
GATE_PAD = N_KV_GROUPS * LANES


def _cparams(sem):
    return pltpu.CompilerParams(dimension_semantics=sem, vmem_limit_bytes=VMEM_LIMIT)


def _gelu_tanh(x):
    c = math.sqrt(2.0 / math.pi)
    return x * (0.5 * (1.0 + jnp.tanh(c * (x + 0.044715 * (x * x * x)))))


def _sigmoid(x):
    return 1.0 / (1.0 + jnp.exp(-x))


def _mm_bias_body(a_ref, b_ref, bias_ref, o_ref):
    acc = jnp.dot(a_ref[...].astype(BF16), b_ref[...].astype(BF16), preferred_element_type=F32)
    o_ref[...] = (acc + bias_ref[...]).astype(o_ref.dtype)


def _matmul_bias(a, b, bias, out_dtype=F32, tm=MM_TM, tn=MM_TN):
    m, k = a.shape
    n = b.shape[1]
    tm = min(tm, m)
    tn = min(tn, n)
    assert m % tm == 0 and n % tn == 0
    return pl.pallas_call(
        _mm_bias_body,
        grid=(m // tm, n // tn),
        in_specs=[
            pl.BlockSpec((tm, k), lambda i, j: (i, 0)),
            pl.BlockSpec((k, tn), lambda i, j: (0, j)),
            pl.BlockSpec((1, tn), lambda i, j: (0, j)),
        ],
        out_specs=pl.BlockSpec((tm, tn), lambda i, j: (i, j)),
        out_shape=jax.ShapeDtypeStruct((m, n), out_dtype),
        compiler_params=_cparams(("parallel", "arbitrary")),
        name="mm_bias",
    )(a, b, bias)


def _mm_merge_body(a_ref, b_ref, bias_ref, g0_ref, g1_ref, ya_ref, o_ref):
    acc = jnp.dot(a_ref[...], b_ref[...].astype(BF16), preferred_element_type=F32) + bias_ref[...]
    o_ref[...] = (_sigmoid(g0_ref[...]) * ya_ref[...] + _sigmoid(g1_ref[...]) * acc).astype(o_ref.dtype)


def _matmul_merge(a, b, bias, z, g0_col, g1_col, y_attn, tm=MM_TM, tn=MM_TN):
    m, k = a.shape
    n = b.shape[1]
    assert m % tm == 0 and n % tn == 0 and g0_col % tn == 0 and g1_col % tn == 0
    return pl.pallas_call(
        _mm_merge_body,
        grid=(m // tm, n // tn),
        in_specs=[
            pl.BlockSpec((tm, k), lambda i, j: (i, 0)),
            pl.BlockSpec((k, tn), lambda i, j: (0, j)),
            pl.BlockSpec((1, tn), lambda i, j: (0, j)),
            pl.BlockSpec((tm, tn), lambda i, j: (i, g0_col // tn + j)),
            pl.BlockSpec((tm, tn), lambda i, j: (i, g1_col // tn + j)),
            pl.BlockSpec((tm, tn), lambda i, j: (i, j)),
        ],
        out_specs=pl.BlockSpec((tm, tn), lambda i, j: (i, j)),
        out_shape=jax.ShapeDtypeStruct((m, n), BF16),
        compiler_params=_cparams(("parallel", "arbitrary")),
        name="mm_merge",
    )(a, b, bias, z, z, y_attn)


def _mm_resid_body(a_ref, b_ref, res_ref, gate_ref, o_ref):
    acc = jnp.dot(a_ref[...], b_ref[...].astype(BF16), preferred_element_type=F32)
    o_ref[...] = res_ref[...] + gate_ref[...] * acc


def _matmul_resid(a, b, res, mod6, gate_idx, seq, tm=MM_TM, tn=MM_TN):
    m, k = a.shape
    n = b.shape[1]
    assert m % tm == 0 and n % tn == 0 and seq % tm == 0
    return pl.pallas_call(
        _mm_resid_body,
        grid=(m // tm, n // tn),
        in_specs=[
            pl.BlockSpec((tm, k), lambda i, j: (i, 0)),
            pl.BlockSpec((k, tn), lambda i, j: (0, j)),
            pl.BlockSpec((tm, tn), lambda i, j: (i, j)),
            pl.BlockSpec((None, 1, tn), lambda i, j: ((i * tm // seq) * 6 + gate_idx, 0, j)),
        ],
        out_specs=pl.BlockSpec((tm, tn), lambda i, j: (i, j)),
        out_shape=jax.ShapeDtypeStruct((m, n), F32),
        compiler_params=_cparams(("parallel", "arbitrary")),
        name="mm_resid",
    )(a, b, res, mod6)


def _norm_mod_body(x_ref, g_ref, shift_ref, scale_ref, hb_ref, *maybe_hf_ref):
    x = x_ref[...]
    y = x * lax.rsqrt(jnp.mean(x * x, axis=-1, keepdims=True) + EPS)
    h = (y * g_ref[...]) * (1.0 + scale_ref[...]) + shift_ref[...]
    hb_ref[...] = h.astype(BF16)
    if maybe_hf_ref:
        maybe_hf_ref[0][...] = h


def _norm_mod(x2d, g, mod6, shift_idx, scale_idx, seq, want_f32, tr=ROW_TILE):
    n, d = x2d.shape
    assert n % tr == 0 and seq % tr == 0
    out_shape = [jax.ShapeDtypeStruct((n, d), BF16)]
    out_specs = [pl.BlockSpec((tr, d), lambda i: (i, 0))]
    if want_f32:
        out_shape.append(jax.ShapeDtypeStruct((n, d), F32))
        out_specs.append(pl.BlockSpec((tr, d), lambda i: (i, 0)))
    return pl.pallas_call(
        _norm_mod_body,
        grid=(n // tr,),
        in_specs=[
            pl.BlockSpec((tr, d), lambda i: (i, 0)),
            pl.BlockSpec((1, d), lambda i: (0, 0)),
            pl.BlockSpec((None, 1, d), lambda i: ((i * tr // seq) * 6 + shift_idx, 0, 0)),
            pl.BlockSpec((None, 1, d), lambda i: ((i * tr // seq) * 6 + scale_idx, 0, 0)),
        ],
        out_specs=out_specs,
        out_shape=out_shape,
        compiler_params=_cparams(("parallel",)),
        name="norm_mod",
    )(x2d, g, mod6, mod6)


def _head_norm_body(q_ref, sl_ref, wn_ref, gq_ref, gs_ref, gw_ref, qn_ref, kv_ref):
    def rms(x, g):
        y = x * lax.rsqrt(jnp.mean(x * x, axis=-1, keepdims=True) + EPS)
        return y * g

    qscale = HEAD_DIM ** -0.5
    for c in range(N_HEADS):
        sl = slice(c * HEAD_DIM, (c + 1) * HEAD_DIM)
        qn_ref[:, sl] = (rms(q_ref[:, sl], gq_ref[...]) * qscale).astype(BF16)
    for g in range(N_KV_GROUPS):
        sl = slice(g * HEAD_DIM, (g + 1) * HEAD_DIM)
        sv = slice(KV_COLS + g * HEAD_DIM, KV_COLS + (g + 1) * HEAD_DIM)
        kv_ref[:, sl] = rms(sl_ref[:, sl], gs_ref[...]).astype(BF16)
        kv_ref[:, sv] = sl_ref[:, sv].astype(BF16)
        kv_ref[:, slice(2 * KV_COLS + g * HEAD_DIM, 2 * KV_COLS + (g + 1) * HEAD_DIM)] = (
            rms(wn_ref[:, sl], gw_ref[...]).astype(BF16))
        kv_ref[:, slice(3 * KV_COLS + g * HEAD_DIM, 3 * KV_COLS + (g + 1) * HEAD_DIM)] = (
            wn_ref[:, sv].astype(BF16))


def _head_norm(z, col_q, col_slc, col_win, gq, gs, gw, tr=ROW_TILE):
    n = z.shape[0]
    w2 = 2 * KV_COLS
    assert n % tr == 0 and col_q % Q_COLS == 0 and col_slc % w2 == 0 and col_win % w2 == 0
    return pl.pallas_call(
        _head_norm_body,
        grid=(n // tr,),
        in_specs=[
            pl.BlockSpec((tr, Q_COLS), lambda i: (i, col_q // Q_COLS)),
            pl.BlockSpec((tr, w2), lambda i: (i, col_slc // w2)),
            pl.BlockSpec((tr, w2), lambda i: (i, col_win // w2)),
            pl.BlockSpec((1, HEAD_DIM), lambda i: (0, 0)),
            pl.BlockSpec((1, HEAD_DIM), lambda i: (0, 0)),
            pl.BlockSpec((1, HEAD_DIM), lambda i: (0, 0)),
        ],
        out_specs=[
            pl.BlockSpec((tr, Q_COLS), lambda i: (i, 0)),
            pl.BlockSpec((tr, 4 * KV_COLS), lambda i: (i, 0)),
        ],
        out_shape=[
            jax.ShapeDtypeStruct((n, Q_COLS), BF16),
            jax.ShapeDtypeStruct((n, 4 * KV_COLS), BF16),
        ],
        compiler_params=_cparams(("parallel",)),
        name="head_norm",
    )(z, z, z, gq, gs, gw)


def _compress_body(c_ref, pa_ref, pb_ref, wt_ref, wb_ref, w2_ref, g_ref, o_ref, *, normalize):
    c = c_ref[...]
    top = jnp.dot((c + pa_ref[...]).astype(BF16), wt_ref[...].astype(BF16), preferred_element_type=F32)
    bot = jnp.dot((c + pb_ref[...]).astype(BF16), wb_ref[...].astype(BF16), preferred_element_type=F32)
    n_chunk = c.shape[0]
    pre = top + pltpu.roll(bot, n_chunk - 1, 0)
    out = jnp.dot(_gelu_tanh(pre).astype(BF16), w2_ref[...].astype(BF16), preferred_element_type=F32)
    if normalize:
        out = out * lax.rsqrt(jnp.mean(out * out, axis=-1, keepdims=True) + EPS) * g_ref[...]
    o_ref[...] = out.astype(BF16)


def _compress(chunks, pos, w1, w2, gain, normalize):
    b, g, n_chunk, ck = chunks.shape
    half = STRIDE_CMP * HEAD_DIM
    pa = pos[:STRIDE_CMP].reshape(1, half)
    pb = pos[STRIDE_CMP:].reshape(1, half)
    full = lambda *shape: pl.BlockSpec(shape, lambda bi, gi: (0,) * len(shape))
    return pl.pallas_call(
        functools.partial(_compress_body, normalize=normalize),
        grid=(b, g),
        in_specs=[
            pl.BlockSpec((None, None, n_chunk, ck), lambda bi, gi: (bi, gi, 0, 0)),
            full(1, half), full(1, half),
            full(half, HEAD_DIM), full(half, HEAD_DIM), full(HEAD_DIM, HEAD_DIM), full(1, HEAD_DIM),
        ],
        out_specs=pl.BlockSpec((None, None, n_chunk, HEAD_DIM), lambda bi, gi: (bi, gi, 0, 0)),
        out_shape=jax.ShapeDtypeStruct((b, g, n_chunk, HEAD_DIM), BF16),
        compiler_params=_cparams(("parallel", "parallel")),
        name="compress",
    )(chunks, pa, pb, w1[:half], w1[half:], w2, gain)


def _nsa_body(q_ref, kc_ref, vc_ref, ks_ref, vs_ref, kw_ref, vw_ref, g_ref, o_ref,
              qs_ref, sc_ref, oc_ref, m_ref, l_ref, a_ref, *, tq, tk, seq):
    n_cmp = (seq - L_CMP) // STRIDE_CMP + 1
    n_blk = seq // L_SEL
    n_sel = min(N_SEL, n_blk)
    ncp = kc_ref.shape[0]
    hpg = HEADS_PER_GROUP
    wk = WINDOW + tq
    nt_dims = (((1,), (1,)), ((), ()))
    masked = -NEG_BIG
    m_floor = -0.1 * NEG_BIG

    t0 = pl.program_id(2) * tq
    t_col = t0 + lax.broadcasted_iota(I32, (tq, 1), 0)

    for h in range(hpg):
        qs_ref[h * tq:(h + 1) * tq, :] = q_ref[:, h * HEAD_DIM:(h + 1) * HEAD_DIM]
    q_all = qs_ref[...]

    def per_head(x):
        return jnp.concatenate([x] * hpg, axis=0)

    def softmax_terms(s):
        m = jnp.maximum(jnp.max(s, axis=-1, keepdims=True), m_floor)
        e = jnp.exp(s - m)
        return e, jnp.sum(e, axis=-1, keepdims=True)

    gates = _sigmoid(g_ref[...])
    gate_col = lambda c: jnp.concatenate([gates[:, 3 * h + c:3 * h + c + 1] for h in range(hpg)], axis=0)

    n_row = lax.broadcasted_iota(I32, (1, ncp), 1)
    vis = (n_row * STRIDE_CMP + (L_CMP - 1) <= t_col) & (n_row < n_cmp)
    s = lax.dot_general(q_all, kc_ref[...], nt_dims, preferred_element_type=F32)
    e, l = softmax_terms(s + per_head(jnp.where(vis, 0.0, masked)))
    p = e / jnp.maximum(l, 1e-30)
    oc_ref[...] = gate_col(0) * jnp.dot(p.astype(BF16), vc_ref[...], preferred_element_type=F32)
    psum = p[0:tq]
    for h in range(1, hpg):
        psum = psum + p[h * tq:(h + 1) * tq]

    w0 = pl.multiple_of(jnp.minimum(jnp.maximum(t0 - WINDOW, 0), seq - wk), tq)
    dist = t_col - (w0 + lax.broadcasted_iota(I32, (1, wk), 1))
    bias_w = jnp.where(dist >= 0, jnp.where(dist < WINDOW, 0.0, masked), masked)
    s = lax.dot_general(q_all, kw_ref[pl.ds(w0, wk), :], nt_dims, preferred_element_type=F32)
    e, l = softmax_terms(s + per_head(bias_w))
    o_win = jnp.dot(e.astype(BF16), vw_ref[pl.ds(w0, wk), :], preferred_element_type=F32)
    oc_ref[...] = oc_ref[...] + gate_col(2) * (o_win / l)

    j_row = lax.broadcasted_iota(I32, (1, LANES), 1)
    n_colv = lax.broadcasted_iota(I32, (ncp, 1), 0)
    overlap = ((n_colv * STRIDE_CMP < j_row * L_SEL + L_SEL)
               & (n_colv * STRIDE_CMP + L_CMP > j_row * L_SEL)
               & (n_colv < n_cmp) & (j_row < n_blk))
    overlap = jnp.where(overlap, 1.0, 0.0).astype(BF16)
    p_hi = psum.astype(BF16)
    r1 = psum - p_hi.astype(F32)
    p_mid = r1.astype(BF16)
    p_lo = (r1 - p_mid.astype(F32)).astype(BF16)
    imp = (jnp.dot(p_hi, overlap, preferred_element_type=F32)
           + jnp.dot(p_mid, overlap, preferred_element_type=F32)
           + jnp.dot(p_lo, overlap, preferred_element_type=F32))

    sel_shift = L_SEL.bit_length() - 1
    cur = jnp.right_shift(t_col, sel_shift)
    forced = (j_row == 0) | (j_row == cur) | (j_row == cur - 1)
    score = jnp.where(forced, NEG_BIG, jnp.where(j_row * L_SEL <= t_col, imp, -NEG_BIG))
    sc_ref[...] = jnp.transpose(score)
    n_grp = n_blk // SUBLANES
    groups = [sc_ref[g * SUBLANES:(g + 1) * SUBLANES, :] for g in range(n_grp)]
    j_sub = lax.broadcasted_iota(I32, (SUBLANES, tq), 0)
    ranks = [jnp.zeros((SUBLANES, tq), F32) for _ in range(n_grp)]
    for i in range(n_blk):
        rb = jnp.broadcast_to(sc_ref[i:i + 1, :], (SUBLANES, tq))
        for g in range(n_grp):
            lo = g * SUBLANES
            if lo > i:
                beats = jnp.where(rb >= groups[g], 1.0, 0.0)
            elif lo + SUBLANES - 1 <= i:
                beats = jnp.where(rb > groups[g], 1.0, 0.0)
            else:
                beats = jnp.where(j_sub + lo > i, jnp.where(rb >= groups[g], 1.0, 0.0),
                                  jnp.where(rb > groups[g], 1.0, 0.0))
            ranks[g] = ranks[g] + beats
    sel_t = jnp.concatenate([jnp.where(r < n_sel, 1.0, 0.0) for r in ranks]
                            + [jnp.zeros((LANES - n_blk, tq), F32)], axis=0)
    sel = jnp.transpose(sel_t).astype(BF16)

    c_row = lax.broadcasted_iota(I32, (1, tk), 1)
    j_colv = lax.broadcasted_iota(I32, (LANES, 1), 0)
    m_ref[...] = jnp.full(m_ref.shape, m_floor, F32)
    l_ref[...] = jnp.zeros(l_ref.shape, F32)
    a_ref[...] = jnp.zeros(a_ref.shape, F32)

    def slc_step(kb, carry):
        k0 = pl.multiple_of(kb * tk, tk)
        kpos = k0 + c_row
        expand = jnp.where(j_colv == jnp.right_shift(kpos, sel_shift), 1.0, 0.0).astype(BF16)
        chosen = jnp.dot(sel, expand, preferred_element_type=F32)
        bias = jnp.where(kpos <= t_col, jnp.where(chosen > 0.5, 0.0, masked), masked)
        s = lax.dot_general(q_all, ks_ref[pl.ds(k0, tk), :], nt_dims, preferred_element_type=F32)
        s = s + per_head(bias)
        m_old = m_ref[...]
        m_new = jnp.maximum(m_old, jnp.max(s, axis=-1, keepdims=True))
        p = jnp.exp(s - m_new)
        alpha = jnp.exp(m_old - m_new)
        l_ref[...] = alpha * l_ref[...] + jnp.sum(p, axis=-1, keepdims=True)
        a_ref[...] = alpha * a_ref[...] + jnp.dot(p.astype(BF16), vs_ref[pl.ds(k0, tk), :],
                                                  preferred_element_type=F32)
        m_ref[...] = m_new
        return carry

    lax.fori_loop(0, lax.div(t0 + (tq + tk - 1), tk), slc_step, 0)

    out = oc_ref[...] + gate_col(1) * (a_ref[...] / l_ref[...])
    for h in range(hpg):
        o_ref[:, h * HEAD_DIM:(h + 1) * HEAD_DIM] = out[h * tq:(h + 1) * tq].astype(BF16)


def _nsa(qn, kvn, kcn, vcm, z, col_gate, batch, seq, tq=NSA_TQ, tk=NSA_TK):
    n = qn.shape[0]
    g = N_KV_GROUPS
    gw = HEADS_PER_GROUP * HEAD_DIM
    nq = seq // tq
    ncp = kcn.shape[2]
    rows = HEADS_PER_GROUP * tq
    assert seq % tq == 0 and seq % tk == 0 and col_gate % LANES == 0
    assert seq // L_SEL <= LANES and (seq // L_SEL) % SUBLANES == 0 and seq >= WINDOW + tq
    kv_spec = lambda part: pl.BlockSpec((seq, HEAD_DIM), lambda b, gi, i: (b, part * g + gi))
    cmp_spec = pl.BlockSpec((None, None, ncp, HEAD_DIM), lambda b, gi, i: (b, gi, 0, 0))
    stat = pltpu.VMEM((rows, 1), F32)
    accs = pltpu.VMEM((rows, HEAD_DIM), F32)
    return pl.pallas_call(
        functools.partial(_nsa_body, tq=tq, tk=tk, seq=seq),
        grid=(batch, g, nq),
        in_specs=[
            pl.BlockSpec((tq, gw), lambda b, gi, i: (b * nq + i, gi)),
            cmp_spec, cmp_spec,
            kv_spec(0), kv_spec(1), kv_spec(2), kv_spec(3),
            pl.BlockSpec((tq, LANES), lambda b, gi, i: (b * nq + i, col_gate // LANES + gi)),
        ],
        out_specs=pl.BlockSpec((tq, gw), lambda b, gi, i: (b * nq + i, gi)),
        out_shape=jax.ShapeDtypeStruct((n, Q_COLS), BF16),
        scratch_shapes=[pltpu.VMEM((rows, HEAD_DIM), BF16), pltpu.VMEM((LANES, tq), F32), accs, stat, stat, accs],
        compiler_params=_cparams(("parallel", "parallel", "arbitrary")),
        name="nsa",
    )(qn, kcn, vcm, kvn, kvn, kvn, kvn, z)


def _conv_body(u_ref, v_ref, hu_ref, hv_ref, w_ref, b_ref, lg_ref, lb_ref, o_ref, ubuf, ybuf, *, ts, halo):
    i = pl.program_id(1)
    hal = hu_ref[...] * _sigmoid(hv_ref[...])
    ubuf[0:halo, :] = jnp.where(i > 0, hal, 0.0)
    ubuf[halo:halo + ts, :] = u_ref[...] * _sigmoid(v_ref[...])
    ch = u_ref.shape[1]
    off = halo - (CONV_WIDTH - 1)
    for c in range(ch // LANES):
        sl = slice(c * LANES, (c + 1) * LANES)
        acc = jnp.zeros((ts, LANES), F32)
        for k in range(CONV_WIDTH):
            acc = acc + ubuf[off + k:off + k + ts, sl] * w_ref[k:k + 1, sl]
        ybuf[:, sl] = acc + b_ref[:, sl]
    y = ybuf[...]
    mu = jnp.mean(y, axis=-1, keepdims=True)
    yc = y - mu
    yn = yc * lax.rsqrt(jnp.mean(yc * yc, axis=-1, keepdims=True) + EPS) * lg_ref[...] + lb_ref[...]
    o_ref[...] = (yn * _sigmoid(yn)).astype(BF16)


def _conv_module(z, col_u, col_v, dw_w, dw_b, ln_g, ln_b, batch, seq, ts=CONV_TS, halo=CONV_HALO):
    n = z.shape[0]
    ch = dw_w.shape[1]
    ns = seq // ts
    r = ts // halo
    assert seq % ts == 0 and ts % halo == 0 and col_u % ch == 0 and col_v % ch == 0
    halo_map = lambda col: (lambda b, i: (jnp.maximum((b * ns + i) * r - 1, 0), col // ch))
    full = lambda *shape: pl.BlockSpec(shape, lambda b, i: (0,) * len(shape))
    return pl.pallas_call(
        functools.partial(_conv_body, ts=ts, halo=halo),
        grid=(batch, ns),
        in_specs=[
            pl.BlockSpec((ts, ch), lambda b, i: (b * ns + i, col_u // ch)),
            pl.BlockSpec((ts, ch), lambda b, i: (b * ns + i, col_v // ch)),
            pl.BlockSpec((halo, ch), halo_map(col_u)),
            pl.BlockSpec((halo, ch), halo_map(col_v)),
            full(CONV_WIDTH, ch), full(1, ch), full(1, ch), full(1, ch),
        ],
        out_specs=pl.BlockSpec((ts, ch), lambda b, i: (b * ns + i, 0)),
        out_shape=jax.ShapeDtypeStruct((n, ch), BF16),
        scratch_shapes=[pltpu.VMEM((ts + halo, ch), F32), pltpu.VMEM((ts, ch), F32)],
        compiler_params=_cparams(("parallel", "arbitrary")),
        name="conv_module",
    )(z, z, z, z, dw_w, dw_b, ln_g, ln_b)


def _peer_topk_body(q_ref, sk_ref, e_ref, g_ref, work, v1, i1, v2, i2, cand, candi, ts_buf, te_buf):
    k = PEER_TOPK
    tmn = q_ref.shape[0]
    nt_dims = (((1,), (1,)), ((), ()))

    def extract(src, nrows, val_out, idx_out, payload=None):
        rows = lax.broadcasted_iota(I32, (nrows, tmn), 0).astype(F32)

        def body(r, carry):
            x = src[0:nrows, :]
            m = jnp.max(x, axis=0, keepdims=True)
            pos = jnp.min(jnp.where(x == m, rows, float(nrows)), axis=0, keepdims=True)
            hit = rows == pos
            val_out[pl.ds(r, 1), :] = m
            if payload is None:
                idx_out[pl.ds(r, 1), :] = pos
            else:
                idx_out[pl.ds(r, 1), :] = jnp.max(jnp.where(hit, payload[...], -1.0), axis=0, keepdims=True)
            src[0:nrows, :] = jnp.where(hit, -jnp.inf, x)
            return carry

        lax.fori_loop(0, k, body, 0)

    for p, (vo, io) in enumerate(((v1, i1), (v2, i2))):
        qh = q_ref[:, p * PEER_HALF:(p + 1) * PEER_HALF].astype(BF16)
        work[0:PEER_NKEYS, :] = lax.dot_general(sk_ref[p].astype(BF16), qh, nt_dims,
                                                preferred_element_type=F32)
        extract(work, PEER_NKEYS, vo, io)

    for a in range(k):
        cand[a * k:(a + 1) * k, :] = v1[a:a + 1, :] + v2[...]
        candi[a * k:(a + 1) * k, :] = i1[a:a + 1, :] * float(PEER_NKEYS) + i2[...]
    extract(cand, k * k, ts_buf, te_buf, payload=candi)

    ts = ts_buf[...]
    ex = jnp.exp(ts - jnp.max(ts, axis=0, keepdims=True))
    g_ref[...] = ex / jnp.sum(ex, axis=0, keepdims=True)
    e_ref[...] = te_buf[...].astype(I32)


def _peer_topk(qp, sub_keys, tmn=PEER_TM):
    n = qp.shape[0]
    k = PEER_TOPK
    qd = 2 * PEER_HALF
    sk = sub_keys.reshape(PEER_HEADS, 2, PEER_NKEYS, PEER_HALF)
    assert n % tmn == 0
    rows = PEER_HEADS * k
    small_f = pltpu.VMEM((k, tmn), F32)
    return pl.pallas_call(
        _peer_topk_body,
        grid=(n // tmn, PEER_HEADS),
        in_specs=[
            pl.BlockSpec((tmn, qd), lambda i, h: (i, h)),
            pl.BlockSpec((None, 2, PEER_NKEYS, PEER_HALF), lambda i, h: (h, 0, 0, 0)),
        ],
        out_specs=[
            pl.BlockSpec((k, tmn), lambda i, h: (h, i)),
            pl.BlockSpec((k, tmn), lambda i, h: (h, i)),
        ],
        out_shape=[
            jax.ShapeDtypeStruct((rows, n), I32),
            jax.ShapeDtypeStruct((rows, n), F32),
        ],
        scratch_shapes=[
            pltpu.VMEM((PEER_NKEYS, tmn), F32),
            small_f, small_f, small_f, small_f,
            pltpu.VMEM((k * k, tmn), F32), pltpu.VMEM((k * k, tmn), F32),
            small_f, small_f,
        ],
        compiler_params=_cparams(("parallel", "arbitrary")),
        name="peer_topk",
    )(qp, sk)


def _peer_mix_body(idx_cur, idx_nxt, uv_hbm, h_ref, gate_ref, x_ref, g2_ref, o_ref,
                   buf_a, buf_b, cbuf, sem, *, tb, ne):
    i = pl.program_id(0)
    nb = pl.num_programs(0)
    half = h_ref.shape[1]
    bufs = (buf_a, buf_b)

    def row_copy(e_idx, which, row):
        return pltpu.make_async_copy(uv_hbm.at[e_idx], bufs[which].at[row], sem.at[which])

    def block_wait(which):
        pltpu.make_async_copy(uv_hbm.at[pl.ds(0, tb * ne)], bufs[which], sem.at[which]).wait()

    def issue(idx_ref, row0, which, t):
        for e in range(ne):
            row_copy(idx_ref[row0 + t, e], which, t * ne + e).start(priority=e % 2)

    @pl.when(i == 0)
    def _():
        def issue_first(t, carry):
            issue(idx_cur, 0, 0, t)
            return carry
        lax.fori_loop(0, tb, issue_first, 0)

    lane = lax.broadcasted_iota(I32, (SUBLANES, LANES), 1)

    def phase(which, tok0, idx_ref, idx_row0):
        src = bufs[which]

        def u_side(t):
            h = h_ref[tok0 + t]
            acc = jnp.zeros((SUBLANES, LANES), F32)
            for e in range(ne):
                prod = src[t * ne + e, 0:half, :].astype(F32) * h
                part = prod[0:SUBLANES]
                for s in range(1, half // SUBLANES):
                    part = part + prod[s * SUBLANES:(s + 1) * SUBLANES]
                acc = jnp.where(lane == e, jnp.sum(part, axis=-1, keepdims=True), acc)
            a = jnp.sum(acc, axis=0, keepdims=True)
            coef = _gelu_tanh(a) * gate_ref[pl.ds(tok0 + t, 1), :]
            cbuf[t % 2] = jnp.transpose(jnp.broadcast_to(coef, (ne, ne)))

        def v_side(t):
            out = jnp.zeros((half, LANES), F32)
            for e in range(ne):
                out = out + cbuf[t % 2, e:e + 1, :] * src[t * ne + e, half:2 * half, :].astype(F32)
            o_ref[tok0 + t] = x_ref[tok0 + t] + g2_ref[...] * out

        block_wait(which)
        u_side(0)
        for t in range(tb - 1):
            issue(idx_ref, idx_row0, 1 - which, t)
            v_side(t)
            u_side(t + 1)
        issue(idx_ref, idx_row0, 1 - which, tb - 1)
        v_side(tb - 1)

    phase(0, 0, idx_cur, tb)
    phase(1, tb, idx_nxt, 0)

    @pl.when(i == nb - 1)
    def _():
        block_wait(0)


def _peer_mix(idx, gate, uv, h2, x1, mod6, gate_idx, seq, tb=PEER_TB):
    n, ne = idx.shape
    d = h2.shape[1]
    half = d // LANES
    tb2 = 2 * tb
    nb = n // tb2
    assert n % tb2 == 0 and seq % tb2 == 0 and ne == LANES
    h3 = h2.reshape(n, half, LANES)
    x3 = x1.reshape(n, half, LANES)
    m3 = mod6.reshape(mod6.shape[0], half, LANES)
    rows = pltpu.VMEM((tb * ne, 2 * half, LANES), BF16)
    out = pl.pallas_call(
        functools.partial(_peer_mix_body, tb=tb, ne=ne),
        grid=(nb,),
        in_specs=[
            pl.BlockSpec((tb2, ne), lambda i: (i, 0), memory_space=pltpu.SMEM),
            pl.BlockSpec((tb2, ne), lambda i: (jnp.minimum(i + 1, nb - 1), 0), memory_space=pltpu.SMEM),
            pl.BlockSpec(memory_space=pl.ANY),
            pl.BlockSpec((tb2, half, LANES), lambda i: (i, 0, 0)),
            pl.BlockSpec((tb2, ne), lambda i: (i, 0)),
            pl.BlockSpec((tb2, half, LANES), lambda i: (i, 0, 0)),
            pl.BlockSpec((None, half, LANES), lambda i: ((i * tb2 // seq) * 6 + gate_idx, 0, 0)),
        ],
        out_specs=pl.BlockSpec((tb2, half, LANES), lambda i: (i, 0, 0)),
        out_shape=jax.ShapeDtypeStruct((n, half, LANES), F32),
        scratch_shapes=[rows, rows, pltpu.VMEM((2, ne, ne), F32), pltpu.SemaphoreType.DMA((2,))],
        compiler_params=_cparams(("arbitrary",)),
        name="peer_mix",
    )(idx, idx, uv, h3, gate, x3, m3)
    return out.reshape(n, d)


def _layer(x, c, norm1_g, norm2_g, w_ada, b_ada, w_in, b_in, q_norm_g, k_norm_g,
           cmp_pos_k, cmp_pos_v, cmp_k_w1, cmp_k_w2, cmp_v_w1, cmp_v_w2, w_nsa_out,
           conv_dw_w, conv_dw_b, conv_ln_g, conv_ln_b, conv_pw_w, conv_pw_b, w_out,
           peer_w_q, peer_sub_keys, peer_u, peer_v):
    b, s, d = x.shape
    n = b * s
    g = N_KV_GROUPS
    conv_ch = conv_dw_w.shape[1]
    x2 = x.reshape(n, d)

    rows = max(SUBLANES, b)
    sc = jnp.zeros((rows, d), F32).at[:b].set(c * _sigmoid(c))
    mod = _matmul_bias(sc, w_ada, b_ada.reshape(1, -1), tm=rows)[:b]
    mod6 = mod.reshape(b * 6, 1, d)

    sizes = (Q_COLS,) + (KV_COLS,) * 6 + (3 * N_HEADS, 2 * conv_ch, 2 * d)
    offs = [0]
    for sz in sizes:
        offs.append(offs[-1] + sz)
    part = lambda arr, k: arr[..., offs[k]:offs[k + 1]]
    per_group = 3 * HEADS_PER_GROUP

    def gate_cols(arr):
        a = part(arr, 7).reshape(arr.shape[:-1] + (g, per_group))
        a = jnp.pad(a, [(0, 0)] * (a.ndim - 1) + [(0, LANES - per_group)])
        return a.reshape(arr.shape[:-1] + (GATE_PAD,))

    order = [0, 8, 9, 1, 2, 3, 4, 5, 6]
    w_cat = jnp.concatenate([part(w_in, k) for k in order] + [gate_cols(w_in)], axis=-1).astype(BF16)
    b_cat = jnp.concatenate([part(b_in, k) for k in order] + [gate_cols(b_in)], axis=-1).reshape(1, -1)
    col_q = 0
    col_glu = Q_COLS
    col_merge = col_glu + 2 * conv_ch
    col_cmp = col_merge + 2 * d
    col_slc = col_cmp + 2 * KV_COLS
    col_win = col_slc + 2 * KV_COLS
    col_gate = col_win + 2 * KV_COLS

    h1, = _norm_mod(x2, norm1_g.reshape(1, d), mod6, 0, 1, s, want_f32=False)
    z = _matmul_bias(h1, w_cat, b_cat)

    qn, kvn = _head_norm(z, col_q, col_slc, col_win, q_norm_g.reshape(1, -1),
                         k_norm_g[1].reshape(1, -1), k_norm_g[2].reshape(1, -1))
    n_chunk = s // STRIDE_CMP

    def chunks(col):
        a = z[:, col:col + KV_COLS].reshape(b, n_chunk, STRIDE_CMP, g, HEAD_DIM)
        return a.transpose(0, 3, 1, 2, 4).reshape(b, g, n_chunk, STRIDE_CMP * HEAD_DIM)

    kcn = _compress(chunks(col_cmp), cmp_pos_k, cmp_k_w1, cmp_k_w2, k_norm_g[0].reshape(1, -1), True)
    vcm = _compress(chunks(col_cmp + KV_COLS), cmp_pos_v, cmp_v_w1, cmp_v_w2, k_norm_g[0].reshape(1, -1), False)

    attn = _nsa(qn, kvn, kcn, vcm, z, col_gate, b, s)
    y_attn = _matmul_bias(attn, w_nsa_out, jnp.zeros((1, d), F32))

    yc = _conv_module(z, col_glu, col_glu + conv_ch, conv_dw_w, conv_dw_b.reshape(1, -1),
                      conv_ln_g.reshape(1, -1), conv_ln_b.reshape(1, -1), b, s)
    merged = _matmul_merge(yc, conv_pw_w, conv_pw_b.reshape(1, -1), z, col_merge, col_merge + d, y_attn)
    x1 = _matmul_resid(merged, w_out, x2, mod6, 2, s)

    h2b, h2f = _norm_mod(x1, norm2_g.reshape(1, d), mod6, 3, 4, s, want_f32=True)
    qp = _matmul_bias(h2b, peer_w_q, jnp.zeros((1, peer_w_q.shape[1]), F32))
    e_t, g_t = _peer_topk(qp, peer_sub_keys)
    idx = e_t.T
    gate = g_t.T
    half = d // LANES
    uv = jnp.concatenate([peer_u.astype(BF16).reshape(-1, half, LANES),
                          peer_v.astype(BF16).reshape(-1, half, LANES)], axis=1)
    out = _peer_mix(idx, gate, uv, h2f, x1, mod6, 5, s)
    return out.reshape(b, s, d)


def kernel(x, c, norm1_g, norm2_g, w_ada, b_ada, w_in, b_in, q_norm_g, k_norm_g, cmp_pos_k, cmp_pos_v,
           cmp_k_w1, cmp_k_w2, cmp_v_w1, cmp_v_w2, w_nsa_out, conv_dw_w, conv_dw_b, conv_ln_g, conv_ln_b,
           conv_pw_w, conv_pw_b, w_out, peer_w_q, peer_sub_keys, peer_u, peer_v):
    depth = norm1_g.shape[0]
    for l in range(depth):
        x = _layer(x, c, norm1_g[l], norm2_g[l], w_ada[l], b_ada[l], w_in[l], b_in[l],
                   q_norm_g[l], k_norm_g[l], cmp_pos_k[l], cmp_pos_v[l],
                   cmp_k_w1[l], cmp_k_w2[l], cmp_v_w1[l], cmp_v_w2[l], w_nsa_out[l],
                   conv_dw_w[l], conv_dw_b[l], conv_ln_g[l], conv_ln_b[l],
                   conv_pw_w[l], conv_pw_b[l], w_out[l],
                   peer_w_q[l], peer_sub_keys[l], peer_u[l], peer_v[l])
    return x
```

```python
import functools
import math

import jax
import jax.numpy as jnp
from jax import lax
from jax.experimental import pallas as pl
from jax.experimental.pallas import tpu as pltpu

F32 = jnp.float32
BF16 = jnp.bfloat16
I32 = jnp.int32

N_HEADS = 16
N_KV_GROUPS = 4
HEADS_PER_GROUP = N_HEADS // N_KV_GROUPS
HEAD_DIM = 128
L_CMP = 32
STRIDE_CMP = 16
L_SEL = 64
N_SEL = 16
WINDOW = 512
CONV_WIDTH = 31
PEER_HEADS = 8
PEER_NKEYS = 128
PEER_HALF = 128
PEER_TOPK = 16
EPS = 1e-6
NEG_BIG = 1e30

LANES = 128
SUBLANES = 8
VMEM_LIMIT = 48 * 1024 * 1024

MM_TM = 1024
MM_TN = 512
ROW_TILE = 256
NSA_TQ = 128
NSA_TK = 1024
CONV_TS = 256
CONV_HALO = 32
PEER_TM = 256
PEER_TB = 8

Q_COLS = N_HEADS * HEAD_DIM
KV_COLS = N_KV_GROUPS * HEAD_DIM
GATE_PAD = N_KV_GROUPS * LANES


def _cparams(sem):
    return pltpu.CompilerParams(dimension_semantics=sem, vmem_limit_bytes=VMEM_LIMIT)


def _gelu_tanh(x):
    c = math.sqrt(2.0 / math.pi)
    return x * (0.5 * (1.0 + jnp.tanh(c * (x + 0.044715 * (x * x * x)))))


def _sigmoid(x):
    return 1.0 / (1.0 + jnp.exp(-x))


def _mm_bias_body(a_ref, b_ref, bias_ref, o_ref):
    acc = jnp.dot(a_ref[...].astype(BF16), b_ref[...].astype(BF16), preferred_element_type=F32)
    o_ref[...] = (acc + bias_ref[...]).astype(o_ref.dtype)


def _matmul_bias(a, b, bias, out_dtype=F32, tm=MM_TM, tn=MM_TN):
    m, k = a.shape
    n = b.shape[1]
    tm = min(tm, m)
    tn = min(tn, n)
    assert m % tm == 0 and n % tn == 0
    return pl.pallas_call(
        _mm_bias_body,
        grid=(m // tm, n // tn),
        in_specs=[
            pl.BlockSpec((tm, k), lambda i, j: (i, 0)),
            pl.BlockSpec((k, tn), lambda i, j: (0, j)),
            pl.BlockSpec((1, tn), lambda i, j: (0, j)),
        ],
        out_specs=pl.BlockSpec((tm, tn), lambda i, j: (i, j)),
        out_shape=jax.ShapeDtypeStruct((m, n), out_dtype),
        compiler_params=_cparams(("parallel", "arbitrary")),
        name="mm_bias",
    )(a, b, bias)


def _mm_merge_body(a_ref, b_ref, bias_ref, g0_ref, g1_ref, ya_ref, o_ref):
    acc = jnp.dot(a_ref[...], b_ref[...].astype(BF16), preferred_element_type=F32) + bias_ref[...]
    o_ref[...] = (_sigmoid(g0_ref[...]) * ya_ref[...] + _sigmoid(g1_ref[...]) * acc).astype(o_ref.dtype)


def _matmul_merge(a, b, bias, z, g0_col, g1_col, y_attn, tm=MM_TM, tn=MM_TN):
    m, k = a.shape
    n = b.shape[1]
    assert m % tm == 0 and n % tn == 0 and g0_col % tn == 0 and g1_col % tn == 0
    return pl.pallas_call(
        _mm_merge_body,
        grid=(m // tm, n // tn),
        in_specs=[
            pl.BlockSpec((tm, k), lambda i, j: (i, 0)),
            pl.BlockSpec((k, tn), lambda i, j: (0, j)),
            pl.BlockSpec((1, tn), lambda i, j: (0, j)),
            pl.BlockSpec((tm, tn), lambda i, j: (i, g0_col // tn + j)),
            pl.BlockSpec((tm, tn), lambda i, j: (i, g1_col // tn + j)),
            pl.BlockSpec((tm, tn), lambda i, j: (i, j)),
        ],
        out_specs=pl.BlockSpec((tm, tn), lambda i, j: (i, j)),
        out_shape=jax.ShapeDtypeStruct((m, n), BF16),
        compiler_params=_cparams(("parallel", "arbitrary")),
        name="mm_merge",
    )(a, b, bias, z, z, y_attn)


def _mm_resid_body(a_ref, b_ref, res_ref, gate_ref, o_ref):
    acc = jnp.dot(a_ref[...], b_ref[...].astype(BF16), preferred_element_type=F32)
    o_ref[...] = res_ref[...] + gate_ref[...] * acc


def _matmul_resid(a, b, res, mod6, gate_idx, seq, tm=MM_TM, tn=MM_TN):
    m, k = a.shape
    n = b.shape[1]
    assert m % tm == 0 and n % tn == 0 and seq % tm == 0
    return pl.pallas_call(
        _mm_resid_body,
        grid=(m // tm, n // tn),
        in_specs=[
            pl.BlockSpec((tm, k), lambda i, j: (i, 0)),
            pl.BlockSpec((k, tn), lambda i, j: (0, j)),
            pl.BlockSpec((tm, tn), lambda i, j: (i, j)),
            pl.BlockSpec((None, 1, tn), lambda i, j: ((i * tm // seq) * 6 + gate_idx, 0, j)),
        ],
        out_specs=pl.BlockSpec((tm, tn), lambda i, j: (i, j)),
        out_shape=jax.ShapeDtypeStruct((m, n), F32),
        compiler_params=_cparams(("parallel", "arbitrary")),
        name="mm_resid",
    )(a, b, res, mod6)


def _norm_mod_body(x_ref, g_ref, shift_ref, scale_ref, hb_ref, *maybe_hf_ref):
    x = x_ref[...]
    y = x * lax.rsqrt(jnp.mean(x * x, axis=-1, keepdims=True) + EPS)
    h = (y * g_ref[...]) * (1.0 + scale_ref[...]) + shift_ref[...]
    hb_ref[...] = h.astype(BF16)
    if maybe_hf_ref:
        maybe_hf_ref[0][...] = h


def _norm_mod(x2d, g, mod6, shift_idx, scale_idx, seq, want_f32, tr=ROW_TILE):
    n, d = x2d.shape
    assert n % tr == 0 and seq % tr == 0
    out_shape = [jax.ShapeDtypeStruct((n, d), BF16)]
    out_specs = [pl.BlockSpec((tr, d), lambda i: (i, 0))]
    if want_f32:
        out_shape.append(jax.ShapeDtypeStruct((n, d), F32))
        out_specs.append(pl.BlockSpec((tr, d), lambda i: (i, 0)))
    return pl.pallas_call(
        _norm_mod_body,
        grid=(n // tr,),
        in_specs=[
            pl.BlockSpec((tr, d), lambda i: (i, 0)),
            pl.BlockSpec((1, d), lambda i: (0, 0)),
            pl.BlockSpec((None, 1, d), lambda i: ((i * tr // seq) * 6 + shift_idx, 0, 0)),
            pl.BlockSpec((None, 1, d), lambda i: ((i * tr // seq) * 6 + scale_idx, 0, 0)),
        ],
        out_specs=out_specs,
        out_shape=out_shape,
        compiler_params=_cparams(("parallel",)),
        name="norm_mod",
    )(x2d, g, mod6, mod6)


def _head_norm_body(q_ref, sl_ref, wn_ref, gq_ref, gs_ref, gw_ref, qn_ref, kv_ref):
    def rms(x, g):
        y = x * lax.rsqrt(jnp.mean(x * x, axis=-1, keepdims=True) + EPS)
        return y * g

    qscale = HEAD_DIM ** -0.5
    for c in range(N_HEADS):
        sl = slice(c * HEAD_DIM, (c + 1) * HEAD_DIM)
        qn_ref[:, sl] = (rms(q_ref[:, sl], gq_ref[...]) * qscale).astype(BF16)
    for g in range(N_KV_GROUPS):
        sl = slice(g * HEAD_DIM, (g + 1) * HEAD_DIM)
        sv = slice(KV_COLS + g * HEAD_DIM, KV_COLS + (g + 1) * HEAD_DIM)
        kv_ref[:, sl] = rms(sl_ref[:, sl], gs_ref[...]).astype(BF16)
        kv_ref[:, sv] = sl_ref[:, sv].astype(BF16)
        kv_ref[:, slice(2 * KV_COLS + g * HEAD_DIM, 2 * KV_COLS + (g + 1) * HEAD_DIM)] = (
            rms(wn_ref[:, sl], gw_ref[...]).astype(BF16))
        kv_ref[:, slice(3 * KV_COLS + g * HEAD_DIM, 3 * KV_COLS + (g + 1) * HEAD_DIM)] = (
            wn_ref[:, sv].astype(BF16))


def _head_norm(z, col_q, col_slc, col_win, gq, gs, gw, tr=ROW_TILE):
    n = z.shape[0]
    w2 = 2 * KV_COLS
    assert n % tr == 0 and col_q % Q_COLS == 0 and col_slc % w2 == 0 and col_win % w2 == 0
    return pl.pallas_call(
        _head_norm_body,
        grid=(n // tr,),
        in_specs=[
            pl.BlockSpec((tr, Q_COLS), lambda i: (i, col_q // Q_COLS)),
            pl.BlockSpec((tr, w2), lambda i: (i, col_slc // w2)),
            pl.BlockSpec((tr, w2), lambda i: (i, col_win // w2)),
            pl.BlockSpec((1, HEAD_DIM), lambda i: (0, 0)),
            pl.BlockSpec((1, HEAD_DIM), lambda i: (0, 0)),
            pl.BlockSpec((1, HEAD_DIM), lambda i: (0, 0)),
        ],
        out_specs=[
            pl.BlockSpec((tr, Q_COLS), lambda i: (i, 0)),
            pl.BlockSpec((tr, 4 * KV_COLS), lambda i: (i, 0)),
        ],
        out_shape=[
            jax.ShapeDtypeStruct((n, Q_COLS), BF16),
            jax.ShapeDtypeStruct((n, 4 * KV_COLS), BF16),
        ],
        compiler_params=_cparams(("parallel",)),
        name="head_norm",
    )(z, z, z, gq, gs, gw)


def _compress_body(c_ref, pa_ref, pb_ref, wt_ref, wb_ref, w2_ref, g_ref, o_ref, *, normalize):
    c = c_ref[...]
    top = jnp.dot((c + pa_ref[...]).astype(BF16), wt_ref[...].astype(BF16), preferred_element_type=F32)
    bot = jnp.dot((c + pb_ref[...]).astype(BF16), wb_ref[...].astype(BF16), preferred_element_type=F32)
    n_chunk = c.shape[0]
    pre = top + pltpu.roll(bot, n_chunk - 1, 0)
    out = jnp.dot(_gelu_tanh(pre).astype(BF16), w2_ref[...].astype(BF16), preferred_element_type=F32)
    if normalize:
        out = out * lax.rsqrt(jnp.mean(out * out, axis=-1, keepdims=True) + EPS) * g_ref[...]
    o_ref[...] = out.astype(BF16)


def _compress(chunks, pos, w1, w2, gain, normalize):
    b, g, n_chunk, ck = chunks.shape
    half = STRIDE_CMP * HEAD_DIM
    pa = pos[:STRIDE_CMP].reshape(1, half)
    pb = pos[STRIDE_CMP:].reshape(1, half)
    full = lambda *shape: pl.BlockSpec(shape, lambda bi, gi: (0,) * len(shape))
    return pl.pallas_call(
        functools.partial(_compress_body, normalize=normalize),
        grid=(b, g),
        in_specs=[
            pl.BlockSpec((None, None, n_chunk, ck), lambda bi, gi: (bi, gi, 0, 0)),
            full(1, half), full(1, half),
            full(half, HEAD_DIM), full(half, HEAD_DIM), full(HEAD_DIM, HEAD_DIM), full(1, HEAD_DIM),
        ],
        out_specs=pl.BlockSpec((None, None, n_chunk, HEAD_DIM), lambda bi, gi: (bi, gi, 0, 0)),
        out_shape=jax.ShapeDtypeStruct((b, g, n_chunk, HEAD_DIM), BF16),
        compiler_params=_cparams(("parallel", "parallel")),
        name="compress",
    )(chunks, pa, pb, w1[:half], w1[half:], w2, gain)


def _nsa_body(q_ref, kc_ref, vc_ref, ks_ref, vs_ref, kw_ref, vw_ref, g_ref, o_ref,
              qs_ref, sc_ref, oc_ref, *, tq, tk, seq):
    n_cmp = (seq - L_CMP) // STRIDE_CMP + 1
    n_blk = seq // L_SEL
    n_sel = min(N_SEL, n_blk)
    ncp = kc_ref.shape[0]
    hpg = HEADS_PER_GROUP
    wk = WINDOW + tq
    nt_dims = (((1,), (1,)), ((), ()))
    masked = -NEG_BIG
    m_floor = -0.1 * NEG_BIG

    t0 = pl.program_id(2) * tq
    t_col = t0 + lax.broadcasted_iota(I32, (tq, 1), 0)

    for h in range(hpg):
        qs_ref[h * tq:(h + 1) * tq, :] = q_ref[:, h * HEAD_DIM:(h + 1) * HEAD_DIM]
    q_all = qs_ref[...]

    def per_head(x):
        return jnp.concatenate([x] * hpg, axis=0)

    def softmax_terms(s):
        m = jnp.maximum(jnp.max(s, axis=-1, keepdims=True), m_floor)
        e = jnp.exp(s - m)
        return e, jnp.sum(e, axis=-1, keepdims=True)

    gates = _sigmoid(g_ref[...])
    gate_col = lambda c: jnp.concatenate([gates[:, 3 * h + c:3 * h + c + 1] for h in range(hpg)], axis=0)

    n_row = lax.broadcasted_iota(I32, (1, ncp), 1)
    vis = (n_row * STRIDE_CMP + (L_CMP - 1) <= t_col) & (n_row < n_cmp)
    s = lax.dot_general(q_all, kc_ref[...], nt_dims, preferred_element_type=F32)
    e, l = softmax_terms(s + per_head(jnp.where(vis, 0.0, masked)))
    p = e / jnp.maximum(l, 1e-30)
    oc_ref[...] = gate_col(0) * jnp.dot(p.astype(BF16), vc_ref[...], preferred_element_type=F32)
    psum = p[0:tq]
    for h in range(1, hpg):
        psum = psum + p[h * tq:(h + 1) * tq]

    w0 = pl.multiple_of(jnp.minimum(jnp.maximum(t0 - WINDOW, 0), seq - wk), tq)
    dist = t_col - (w0 + lax.broadcasted_iota(I32, (1, wk), 1))
    bias_w = jnp.where(dist >= 0, jnp.where(dist < WINDOW, 0.0, masked), masked)
    s = lax.dot_general(q_all, kw_ref[pl.ds(w0, wk), :], nt_dims, preferred_element_type=F32)
    e, l = softmax_terms(s + per_head(bias_w))
    o_win = jnp.dot(e.astype(BF16), vw_ref[pl.ds(w0, wk), :], preferred_element_type=F32)
    oc_ref[...] = oc_ref[...] + gate_col(2) * (o_win / l)

    j_row = lax.broadcasted_iota(I32, (1, LANES), 1)
    n_colv = lax.broadcasted_iota(I32, (ncp, 1), 0)
    overlap = ((n_colv * STRIDE_CMP < j_row * L_SEL + L_SEL)
               & (n_colv * STRIDE_CMP + L_CMP > j_row * L_SEL)
               & (n_colv < n_cmp) & (j_row < n_blk))
    overlap = jnp.where(overlap, 1.0, 0.0).astype(BF16)
    p_hi = psum.astype(BF16)
    r1 = psum - p_hi.astype(F32)
    p_mid = r1.astype(BF16)
    p_lo = (r1 - p_mid.astype(F32)).astype(BF16)
    imp = (jnp.dot(p_hi, overlap, preferred_element_type=F32)
           + jnp.dot(p_mid, overlap, preferred_element_type=F32)
           + jnp.dot(p_lo, overlap, preferred_element_type=F32))

    sel_shift = L_SEL.bit_length() - 1
    cur = jnp.right_shift(t_col, sel_shift)
    forced = (j_row == 0) | (j_row == cur) | (j_row == cur - 1)
    score = jnp.where(forced, NEG_BIG, jnp.where(j_row * L_SEL <= t_col, imp, -NEG_BIG))
    sc_ref[...] = jnp.transpose(score)
    n_grp = n_blk // SUBLANES
    groups = [sc_ref[g * SUBLANES:(g + 1) * SUBLANES, :] for g in range(n_grp)]
    j_sub = lax.broadcasted_iota(I32, (SUBLANES, tq), 0)
    ranks = [jnp.zeros((SUBLANES, tq), F32) for _ in range(n_grp)]
    for i in range(n_blk):
        rb = jnp.broadcast_to(sc_ref[i:i + 1, :], (SUBLANES, tq))
        for g in range(n_grp):
            lo = g * SUBLANES
            if lo > i:
                beats = jnp.where(rb >= groups[g], 1.0, 0.0)
            elif lo + SUBLANES - 1 <= i:
                beats = jnp.where(rb > groups[g], 1.0, 0.0)
            else:
                beats = jnp.where(j_sub + lo > i, jnp.where(rb >= groups[g], 1.0, 0.0),
                                  jnp.where(rb > groups[g], 1.0, 0.0))
            ranks[g] = ranks[g] + beats
    sel_t = jnp.concatenate([jnp.where(r < n_sel, 1.0, 0.0) for r in ranks]
                            + [jnp.zeros((LANES - n_blk, tq), F32)], axis=0)
    sel = jnp.transpose(sel_t).astype(BF16)

    c_row = lax.broadcasted_iota(I32, (1, tk), 1)
    j_colv = lax.broadcasted_iota(I32, (LANES, 1), 0)
    ones_cols = jnp.ones((tk, HEAD_DIM), BF16)
    rows = hpg * tq

    def slc_block(jb, state):
        m_old, l_old, acc_old = state
        k0 = jb * tk
        kpos = k0 + c_row
        expand = jnp.where(j_colv == jnp.right_shift(kpos, sel_shift), 1.0, 0.0).astype(BF16)
        chosen = jnp.dot(sel, expand, preferred_element_type=F32)
        bias = jnp.where(kpos <= t_col, jnp.where(chosen > 0.5, 0.0, masked), masked)
        s = lax.dot_general(q_all, ks_ref[k0:k0 + tk, :], nt_dims, preferred_element_type=F32)
        s = s + per_head(bias)
        m_new = jnp.maximum(m_old, jnp.max(s, axis=-1, keepdims=True))
        p = jnp.exp(s - m_new).astype(BF16)
        alpha = jnp.broadcast_to(jnp.exp(m_old - m_new), (rows, HEAD_DIM))
        pv = jnp.dot(p, jnp.concatenate([vs_ref[k0:k0 + tk, :], ones_cols], axis=1), preferred_element_type=F32)
        return m_new, alpha * l_old + pv[:, HEAD_DIM:], alpha * acc_old + pv[:, :HEAD_DIM]

    n_blocks = lax.div(t0 + (tq + tk - 1), tk)
    for nb in range(1, seq // tk + 1):
        @pl.when(n_blocks == nb)
        def _(nb=nb):
            state = (jnp.full((rows, 1), m_floor, F32), jnp.zeros((rows, HEAD_DIM), F32),
                     jnp.zeros((rows, HEAD_DIM), F32))
            for jb in range(nb):
                state = slc_block(jb, state)
            out = oc_ref[...] + gate_col(1) * (state[2] / state[1])
            for h in range(hpg):
                o_ref[:, h * HEAD_DIM:(h + 1) * HEAD_DIM] = out[h * tq:(h + 1) * tq].astype(BF16)


def _nsa(qn, kvn, kcn, vcm, z, col_gate, batch, seq, tq=NSA_TQ, tk=NSA_TK):
    n = qn.shape[0]
    g = N_KV_GROUPS
    gw = HEADS_PER_GROUP * HEAD_DIM
    nq = seq // tq
    ncp = kcn.shape[2]
    rows = HEADS_PER_GROUP * tq
    assert seq % tq == 0 and seq % tk == 0 and col_gate % LANES == 0
    assert seq // L_SEL <= LANES and (seq // L_SEL) % SUBLANES == 0 and seq >= WINDOW + tq
    kv_spec = lambda part: pl.BlockSpec((seq, HEAD_DIM), lambda b, gi, i: (b, part * g + gi))
    cmp_spec = pl.BlockSpec((None, None, ncp, HEAD_DIM), lambda b, gi, i: (b, gi, 0, 0))
    return pl.pallas_call(
        functools.partial(_nsa_body, tq=tq, tk=tk, seq=seq),
        grid=(batch, g, nq),
        in_specs=[
            pl.BlockSpec((tq, gw), lambda b, gi, i: (b * nq + i, gi)),
            cmp_spec, cmp_spec,
            kv_spec(0), kv_spec(1), kv_spec(2), kv_spec(3),
            pl.BlockSpec((tq, LANES), lambda b, gi, i: (b * nq + i, col_gate // LANES + gi)),
        ],
        out_specs=pl.BlockSpec((tq, gw), lambda b, gi, i: (b * nq + i, gi)),
        out_shape=jax.ShapeDtypeStruct((n, Q_COLS), BF16),
        scratch_shapes=[pltpu.VMEM((rows, HEAD_DIM), BF16), pltpu.VMEM((LANES, tq), F32),
                        pltpu.VMEM((rows, HEAD_DIM), F32)],
        compiler_params=_cparams(("parallel", "parallel", "arbitrary")),
        name="nsa",
    )(qn, kcn, vcm, kvn, kvn, kvn, kvn, z)


def _conv_body(u_ref, v_ref, hu_ref, hv_ref, w_ref, b_ref, lg_ref, lb_ref, o_ref, ubuf, ybuf, *, ts, halo):
    i = pl.program_id(1)
    hal = hu_ref[...] * _sigmoid(hv_ref[...])
    ubuf[0:halo, :] = jnp.where(i > 0, hal, 0.0)
    ubuf[halo:halo + ts, :] = u_ref[...] * _sigmoid(v_ref[...])
    ch = u_ref.shape[1]
    off = halo - (CONV_WIDTH - 1)
    for c in range(ch // LANES):
        sl = slice(c * LANES, (c + 1) * LANES)
        acc = jnp.zeros((ts, LANES), F32)
        for k in range(CONV_WIDTH):
            acc = acc + ubuf[off + k:off + k + ts, sl] * w_ref[k:k + 1, sl]
        ybuf[:, sl] = acc + b_ref[:, sl]
    y = ybuf[...]
    mu = jnp.mean(y, axis=-1, keepdims=True)
    yc = y - mu
    yn = yc * lax.rsqrt(jnp.mean(yc * yc, axis=-1, keepdims=True) + EPS) * lg_ref[...] + lb_ref[...]
    o_ref[...] = (yn * _sigmoid(yn)).astype(BF16)


def _conv_module(z, col_u, col_v, dw_w, dw_b, ln_g, ln_b, batch, seq, ts=CONV_TS, halo=CONV_HALO):
    n = z.shape[0]
    ch = dw_w.shape[1]
    ns = seq // ts
    r = ts // halo
    assert seq % ts == 0 and ts % halo == 0 and col_u % ch == 0 and col_v % ch == 0
    halo_map = lambda col: (lambda b, i: (jnp.maximum((b * ns + i) * r - 1, 0), col // ch))
    full = lambda *shape: pl.BlockSpec(shape, lambda b, i: (0,) * len(shape))
    return pl.pallas_call(
        functools.partial(_conv_body, ts=ts, halo=halo),
        grid=(batch, ns),
        in_specs=[
            pl.BlockSpec((ts, ch), lambda b, i: (b * ns + i, col_u // ch)),
            pl.BlockSpec((ts, ch), lambda b, i: (b * ns + i, col_v // ch)),
            pl.BlockSpec((halo, ch), halo_map(col_u)),
            pl.BlockSpec((halo, ch), halo_map(col_v)),
            full(CONV_WIDTH, ch), full(1, ch), full(1, ch), full(1, ch),
        ],
        out_specs=pl.BlockSpec((ts, ch), lambda b, i: (b * ns + i, 0)),
        out_shape=jax.ShapeDtypeStruct((n, ch), BF16),
        scratch_shapes=[pltpu.VMEM((ts + halo, ch), F32), pltpu.VMEM((ts, ch), F32)],
        compiler_params=_cparams(("parallel", "arbitrary")),
        name="conv_module",
    )(z, z, z, z, dw_w, dw_b, ln_g, ln_b)


def _peer_topk_body(q_ref, sk_ref, e_ref, g_ref, work, v1, i1, v2, i2, cand, candi, ts_buf, te_buf):
    k = PEER_TOPK
    tmn = q_ref.shape[0]
    nt_dims = (((1,), (1,)), ((), ()))

    def extract(src, nrows, val_out, idx_out, payload=None):
        rows = lax.broadcasted_iota(I32, (nrows, tmn), 0).astype(F32)

        def body(r, carry):
            x = src[0:nrows, :]
            m = jnp.max(x, axis=0, keepdims=True)
            pos = jnp.min(jnp.where(x == m, rows, float(nrows)), axis=0, keepdims=True)
            hit = rows == pos
            val_out[pl.ds(r, 1), :] = m
            if payload is None:
                idx_out[pl.ds(r, 1), :] = pos
            else:
                idx_out[pl.ds(r, 1), :] = jnp.max(jnp.where(hit, payload[...], -1.0), axis=0, keepdims=True)
            src[0:nrows, :] = jnp.where(hit, -jnp.inf, x)
            return carry

        lax.fori_loop(0, k, body, 0)

    for p, (vo, io) in enumerate(((v1, i1), (v2, i2))):
        qh = q_ref[:, p * PEER_HALF:(p + 1) * PEER_HALF].astype(BF16)
        work[0:PEER_NKEYS, :] = lax.dot_general(sk_ref[p].astype(BF16), qh, nt_dims,
                                                preferred_element_type=F32)
        extract(work, PEER_NKEYS, vo, io)

    off = 0
    for a in range(k):
        w = k // (a + 1)
        cand[off:off + w, :] = v1[a:a + 1, :] + v2[0:w, :]
        candi[off:off + w, :] = i1[a:a + 1, :] * float(PEER_NKEYS) + i2[0:w, :]
        off += w
    n_rows = cand.shape[0]
    cand[off:n_rows, :] = jnp.full((n_rows - off, tmn), -jnp.inf, F32)
    candi[off:n_rows, :] = jnp.zeros((n_rows - off, tmn), F32)
    extract(cand, n_rows, ts_buf, te_buf, payload=candi)

    ts = ts_buf[...]
    ex = jnp.exp(ts - jnp.max(ts, axis=0, keepdims=True))
    g_ref[...] = ex / jnp.sum(ex, axis=0, keepdims=True)
    e_ref[...] = te_buf[...].astype(I32)


def _peer_topk(qp, sub_keys, tmn=PEER_TM):
    n = qp.shape[0]
    k = PEER_TOPK
    qd = 2 * PEER_HALF
    sk = sub_keys.reshape(PEER_HEADS, 2, PEER_NKEYS, PEER_HALF)
    assert n % tmn == 0
    rows = PEER_HEADS * k
    n_cand = sum(k // (a + 1) for a in range(k))
    cand_rows = -(-n_cand // SUBLANES) * SUBLANES
    small_f = pltpu.VMEM((k, tmn), F32)
    return pl.pallas_call(
        _peer_topk_body,
        grid=(n // tmn, PEER_HEADS),
        in_specs=[
            pl.BlockSpec((tmn, qd), lambda i, h: (i, h)),
            pl.BlockSpec((None, 2, PEER_NKEYS, PEER_HALF), lambda i, h: (h, 0, 0, 0)),
        ],
        out_specs=[
            pl.BlockSpec((k, tmn), lambda i, h: (h, i)),
            pl.BlockSpec((k, tmn), lambda i, h: (h, i)),
        ],
        out_shape=[
            jax.ShapeDtypeStruct((rows, n), I32),
            jax.ShapeDtypeStruct((rows, n), F32),
        ],
        scratch_shapes=[
            pltpu.VMEM((PEER_NKEYS, tmn), F32),
            small_f, small_f, small_f, small_f,
            pltpu.VMEM((cand_rows, tmn), F32), pltpu.VMEM((cand_rows, tmn), F32),
            small_f, small_f,
        ],
        compiler_params=_cparams(("parallel", "arbitrary")),
        name="peer_topk",
    )(qp, sk)


def _peer_mix_body(idx_cur, idx_nxt, uv_hbm, h_ref, gate_ref, x_ref, g2_ref, o_ref,
                   buf_a, buf_b, cbuf, sem, *, tb, ne):
    i = pl.program_id(0)
    nb = pl.num_programs(0)
    half = h_ref.shape[1]
    bufs = (buf_a, buf_b)

    def row_copy(e_idx, which, row):
        return pltpu.make_async_copy(uv_hbm.at[e_idx], bufs[which].at[row], sem.at[which])

    def block_wait(which):
        pltpu.make_async_copy(uv_hbm.at[pl.ds(0, tb * ne)], bufs[which], sem.at[which]).wait()

    def issue(idx_ref, row0, which, t):
        for e in range(ne):
            row_copy(idx_ref[row0 + t, e], which, t * ne + e).start(priority=e % 2)

    @pl.when(i == 0)
    def _():
        def issue_first(t, carry):
            issue(idx_cur, 0, 0, t)
            return carry
        lax.fori_loop(0, tb, issue_first, 0)

    lane = lax.broadcasted_iota(I32, (SUBLANES, LANES), 1)

    def phase(which, tok0, idx_ref, idx_row0):
        src = bufs[which]

        def u_side(t):
            h = h_ref[tok0 + t]
            acc = jnp.zeros((SUBLANES, LANES), F32)
            for e in range(ne):
                prod = src[t * ne + e, 0:half, :].astype(F32) * h
                part = prod[0:SUBLANES]
                for s in range(1, half // SUBLANES):
                    part = part + prod[s * SUBLANES:(s + 1) * SUBLANES]
                acc = jnp.where(lane == e, jnp.sum(part, axis=-1, keepdims=True), acc)
            a = jnp.sum(acc, axis=0, keepdims=True)
            coef = _gelu_tanh(a) * gate_ref[pl.ds(tok0 + t, 1), :]
            cbuf[t % 2] = jnp.transpose(jnp.broadcast_to(coef, (ne, ne)))

        def v_side(t):
            out = jnp.zeros((half, LANES), F32)
            for e in range(ne):
                out = out + cbuf[t % 2, e:e + 1, :] * src[t * ne + e, half:2 * half, :].astype(F32)
            o_ref[tok0 + t] = x_ref[tok0 + t] + g2_ref[...] * out

        block_wait(which)
        u_side(0)
        for t in range(tb - 1):
            issue(idx_ref, idx_row0, 1 - which, t)
            v_side(t)
            u_side(t + 1)
        issue(idx_ref, idx_row0, 1 - which, tb - 1)
        v_side(tb - 1)

    phase(0, 0, idx_cur, tb)
    phase(1, tb, idx_nxt, 0)

    @pl.when(i == nb - 1)
    def _():
        block_wait(0)


def _peer_mix(idx, gate, uv, h2, x1, mod6, gate_idx, seq, tb=PEER_TB):
    n, ne = idx.shape
    d = h2.shape[1]
    half = d // LANES
    tb2 = 2 * tb
    nb = n // tb2
    assert n % tb2 == 0 and seq % tb2 == 0 and ne == LANES
    h3 = h2.reshape(n, half, LANES)
    x3 = x1.reshape(n, half, LANES)
    m3 = mod6.reshape(mod6.shape[0], half, LANES)
    rows = pltpu.VMEM((tb * ne, 2 * half, LANES), BF16)
    out = pl.pallas_call(
        functools.partial(_peer_mix_body, tb=tb, ne=ne),
        grid=(nb,),
        in_specs=[
            pl.BlockSpec((tb2, ne), lambda i: (i, 0), memory_space=pltpu.SMEM),
            pl.BlockSpec((tb2, ne), lambda i: (jnp.minimum(i + 1, nb - 1), 0), memory_space=pltpu.SMEM),
            pl.BlockSpec(memory_space=pl.ANY),
            pl.BlockSpec((tb2, half, LANES), lambda i: (i, 0, 0)),
            pl.BlockSpec((tb2, ne), lambda i: (i, 0)),
            pl.BlockSpec((tb2, half, LANES), lambda i: (i, 0, 0)),
            pl.BlockSpec((None, half, LANES), lambda i: ((i * tb2 // seq) * 6 + gate_idx, 0, 0)),
        ],
        out_specs=pl.BlockSpec((tb2, half, LANES), lambda i: (i, 0, 0)),
        out_shape=jax.ShapeDtypeStruct((n, half, LANES), F32),
        scratch_shapes=[rows, rows, pltpu.VMEM((2, ne, ne), F32), pltpu.SemaphoreType.DMA((2,))],
        compiler_params=_cparams(("arbitrary",)),
        name="peer_mix",
    )(idx, idx, uv, h3, gate, x3, m3)
    return out.reshape(n, d)


def _layer(x, c, norm1_g, norm2_g, w_ada, b_ada, w_in, b_in, q_norm_g, k_norm_g,
           cmp_pos_k, cmp_pos_v, cmp_k_w1, cmp_k_w2, cmp_v_w1, cmp_v_w2, w_nsa_out,
           conv_dw_w, conv_dw_b, conv_ln_g, conv_ln_b, conv_pw_w, conv_pw_b, w_out,
           peer_w_q, peer_sub_keys, peer_u, peer_v):
    b, s, d = x.shape
    n = b * s
    g = N_KV_GROUPS
    conv_ch = conv_dw_w.shape[1]
    x2 = x.reshape(n, d)

    rows = max(SUBLANES, b)
    sc = jnp.zeros((rows, d), F32).at[:b].set(c * _sigmoid(c))
    mod = _matmul_bias(sc, w_ada, b_ada.reshape(1, -1), tm=rows)[:b]
    mod6 = mod.reshape(b * 6, 1, d)

    sizes = (Q_COLS,) + (KV_COLS,) * 6 + (3 * N_HEADS, 2 * conv_ch, 2 * d)
    offs = [0]
    for sz in sizes:
        offs.append(offs[-1] + sz)
    part = lambda arr, k: arr[..., offs[k]:offs[k + 1]]
    per_group = 3 * HEADS_PER_GROUP

    def gate_cols(arr):
        a = part(arr, 7).reshape(arr.shape[:-1] + (g, per_group))
        a = jnp.pad(a, [(0, 0)] * (a.ndim - 1) + [(0, LANES - per_group)])
        return a.reshape(arr.shape[:-1] + (GATE_PAD,))

    order = [0, 8, 9, 1, 2, 3, 4, 5, 6]
    w_cat = jnp.concatenate([part(w_in, k) for k in order] + [gate_cols(w_in)], axis=-1).astype(BF16)
    b_cat = jnp.concatenate([part(b_in, k) for k in order] + [gate_cols(b_in)], axis=-1).reshape(1, -1)
    col_q = 0
    col_glu = Q_COLS
    col_merge = col_glu + 2 * conv_ch
    col_cmp = col_merge + 2 * d
    col_slc = col_cmp + 2 * KV_COLS
    col_win = col_slc + 2 * KV_COLS
    col_gate = col_win + 2 * KV_COLS

    h1, = _norm_mod(x2, norm1_g.reshape(1, d), mod6, 0, 1, s, want_f32=False)
    z = _matmul_bias(h1, w_cat, b_cat)

    qn, kvn = _head_norm(z, col_q, col_slc, col_win, q_norm_g.reshape(1, -1),
                         k_norm_g[1].reshape(1, -1), k_norm_g[2].reshape(1, -1))
    n_chunk = s // STRIDE_CMP

    def chunks(col):
        a = z[:, col:col + KV_COLS].reshape(b, n_chunk, STRIDE_CMP, g, HEAD_DIM)
        return a.transpose(0, 3, 1, 2, 4).reshape(b, g, n_chunk, STRIDE_CMP * HEAD_DIM)

    kcn = _compress(chunks(col_cmp), cmp_pos_k, cmp_k_w1, cmp_k_w2, k_norm_g[0].reshape(1, -1), True)
    vcm = _compress(chunks(col_cmp + KV_COLS), cmp_pos_v, cmp_v_w1, cmp_v_w2, k_norm_g[0].reshape(1, -1), False)

    attn = _nsa(qn, kvn, kcn, vcm, z, col_gate, b, s)
    y_attn = _matmul_bias(attn, w_nsa_out, jnp.zeros((1, d), F32))

    yc = _conv_module(z, col_glu, col_glu + conv_ch, conv_dw_w, conv_dw_b.reshape(1, -1),
                      conv_ln_g.reshape(1, -1), conv_ln_b.reshape(1, -1), b, s)
    merged = _matmul_merge(yc, conv_pw_w, conv_pw_b.reshape(1, -1), z, col_merge, col_merge + d, y_attn)
    x1 = _matmul_resid(merged, w_out, x2, mod6, 2, s)

    h2b, h2f = _norm_mod(x1, norm2_g.reshape(1, d), mod6, 3, 4, s, want_f32=True)
    qp = _matmul_bias(h2b, peer_w_q, jnp.zeros((1, peer_w_q.shape[1]), F32))
    e_t, g_t = _peer_topk(qp, peer_sub_keys)
    idx = e_t.T
    gate = g_t.T
    half = d // LANES
    uv = jnp.concatenate([peer_u.astype(BF16).reshape(-1, half, LANES),
                          peer_v.astype(BF16).reshape(-1, half, LANES)], axis=1)
    out = _peer_mix(idx, gate, uv, h2f, x1, mod6, 5, s)
    return out.reshape(b, s, d)


def kernel(x, c, norm1_g, norm2_g, w_ada, b_ada, w_in, b_in, q_norm_g, k_norm_g, cmp_pos_k, cmp_pos_v,
           cmp_k_w1, cmp_k_w2, cmp_v_w1, cmp_v_w2, w_nsa_out, conv_dw_w, conv_dw_b, conv_ln_g, conv_ln_b,
           conv_pw_w, conv_pw_b, w_out, peer_w_q, peer_sub_keys, peer_u, peer_v):
    depth = norm1_g.shape[0]
    for l in range(depth):
        x = _layer(x, c, norm1_g[l], norm2_g[l], w_ada[l], b_ada[l], w_in[l], b_in[l],
                   q_norm_g[l], k_norm_g[l], cmp_pos_k[l], cmp_pos_v[l],
                   cmp_k_w1[l], cmp_k_w2[l], cmp_v_w1[l], cmp_v_w2[l], w_nsa_out[l],
                   conv_dw_w[l], conv_dw_b[l], conv_ln_g[l], conv_ln_b[l],
                   conv_pw_w[l], conv_pw_b[l], w_out[l],
                   peer_w_q[l], peer_sub_keys[l], peer_u[l], peer_v[l])
    return x
```

```python
import functools
import math

import jax
import jax.numpy as jnp
from jax import lax
from jax.experimental import pallas as pl
from jax.experimental.pallas import tpu as pltpu

F32 = jnp.float32
BF16 = jnp.bfloat16
I32 = jnp.int32

N_HEADS = 16
N_KV_GROUPS = 4
HEADS_PER_GROUP = N_HEADS // N_KV_GROUPS
HEAD_DIM = 128
L_CMP = 32
STRIDE_CMP = 16
L_SEL = 64
N_SEL = 16
WINDOW = 512
CONV_WIDTH = 31
PEER_HEADS = 8
PEER_NKEYS = 128
PEER_HALF = 128
PEER_TOPK = 16
EPS = 1e-6
NEG_BIG = 1e30

LANES = 128
SUBLANES = 8
VMEM_LIMIT = 48 * 1024 * 1024

MM_TM = 1024
MM_TN = 512
ROW_TILE = 256
NSA_TQ = 256
NSA_TK = 1024
CONV_TS = 256
CONV_HALO = 32
PEER_TM = 1024
PEER_TB = 8
UV_TE = 256

Q_COLS = N_HEADS * HEAD_DIM
KV_COLS = N_KV_GROUPS * HEAD_DIM
GATE_PAD = N_KV_GROUPS * LANES


def _cparams(sem):
    return pltpu.CompilerParams(dimension_semantics=sem, vmem_limit_bytes=VMEM_LIMIT)


def _gelu_tanh(x):
    c = math.sqrt(2.0 / math.pi)
    return x * (0.5 * (1.0 + jnp.tanh(c * (x + 0.044715 * (x * x * x)))))


def _sigmoid(x):
    return 1.0 / (1.0 + jnp.exp(-x))


def _mm_bias_body(a_ref, b_ref, bias_ref, o_ref):
    acc = jnp.dot(a_ref[...].astype(BF16), b_ref[...].astype(BF16), preferred_element_type=F32)
    o_ref[...] = (acc + bias_ref[...]).astype(o_ref.dtype)


def _matmul_bias(a, b, bias, out_dtype=F32, tm=MM_TM, tn=MM_TN):
    m, k = a.shape
    n = b.shape[1]
    tm = min(tm, m)
    tn = min(tn, n)
    assert m % tm == 0 and n % tn == 0
    return pl.pallas_call(
        _mm_bias_body,
        grid=(m // tm, n // tn),
        in_specs=[
            pl.BlockSpec((tm, k), lambda i, j: (i, 0)),
            pl.BlockSpec((k, tn), lambda i, j: (0, j)),
            pl.BlockSpec((1, tn), lambda i, j: (0, j)),
        ],
        out_specs=pl.BlockSpec((tm, tn), lambda i, j: (i, j)),
        out_shape=jax.ShapeDtypeStruct((m, n), out_dtype),
        compiler_params=_cparams(("parallel", "arbitrary")),
        name="mm_bias",
    )(a, b, bias)


def _mm_merge_body(a_ref, b_ref, bias_ref, g0_ref, g1_ref, ya_ref, o_ref):
    acc = jnp.dot(a_ref[...], b_ref[...].astype(BF16), preferred_element_type=F32) + bias_ref[...]
    o_ref[...] = (_sigmoid(g0_ref[...]) * ya_ref[...] + _sigmoid(g1_ref[...]) * acc).astype(o_ref.dtype)


def _matmul_merge(a, b, bias, z, g0_col, g1_col, y_attn, tm=MM_TM, tn=MM_TN):
    m, k = a.shape
    n = b.shape[1]
    assert m % tm == 0 and n % tn == 0 and g0_col % tn == 0 and g1_col % tn == 0
    return pl.pallas_call(
        _mm_merge_body,
        grid=(m // tm, n // tn),
        in_specs=[
            pl.BlockSpec((tm, k), lambda i, j: (i, 0)),
            pl.BlockSpec((k, tn), lambda i, j: (0, j)),
            pl.BlockSpec((1, tn), lambda i, j: (0, j)),
            pl.BlockSpec((tm, tn), lambda i, j: (i, g0_col // tn + j)),
            pl.BlockSpec((tm, tn), lambda i, j: (i, g1_col // tn + j)),
            pl.BlockSpec((tm, tn), lambda i, j: (i, j)),
        ],
        out_specs=pl.BlockSpec((tm, tn), lambda i, j: (i, j)),
        out_shape=jax.ShapeDtypeStruct((m, n), BF16),
        compiler_params=_cparams(("parallel", "arbitrary")),
        name="mm_merge",
    )(a, b, bias, z, z, y_attn)


def _mm_resid_body(a_ref, b_ref, res_ref, gate_ref, o_ref):
    acc = jnp.dot(a_ref[...], b_ref[...].astype(BF16), preferred_element_type=F32)
    o_ref[...] = res_ref[...] + gate_ref[...] * acc


def _matmul_resid(a, b, res, mod6, gate_idx, seq, tm=MM_TM, tn=MM_TN):
    m, k = a.shape
    n = b.shape[1]
    assert m % tm == 0 and n % tn == 0 and seq % tm == 0
    return pl.pallas_call(
        _mm_resid_body,
        grid=(m // tm, n // tn),
        in_specs=[
            pl.BlockSpec((tm, k), lambda i, j: (i, 0)),
            pl.BlockSpec((k, tn), lambda i, j: (0, j)),
            pl.BlockSpec((tm, tn), lambda i, j: (i, j)),
            pl.BlockSpec((None, 1, tn), lambda i, j: ((i * tm // seq) * 6 + gate_idx, 0, j)),
        ],
        out_specs=pl.BlockSpec((tm, tn), lambda i, j: (i, j)),
        out_shape=jax.ShapeDtypeStruct((m, n), F32),
        compiler_params=_cparams(("parallel", "arbitrary")),
        name="mm_resid",
    )(a, b, res, mod6)


def _mm_norm_body(x_ref, g_ref, shift_ref, scale_ref, b_ref, bias_ref, o_ref, *rest, want_f32):
    hb_ref = rest[-1]

    @pl.when(pl.program_id(1) == 0)
    def _():
        x = x_ref[...]
        y = x * lax.rsqrt(jnp.mean(x * x, axis=-1, keepdims=True) + EPS)
        h = (y * g_ref[...]) * (1.0 + scale_ref[...]) + shift_ref[...]
        hb_ref[...] = h.astype(BF16)
        if want_f32:
            rest[0][...] = h

    acc = jnp.dot(hb_ref[...], b_ref[...].astype(BF16), preferred_element_type=F32)
    o_ref[...] = acc + bias_ref[...]


def _matmul_norm(x2d, g, mod6, shift_idx, scale_idx, seq, b, bias, want_f32, tm, tn=MM_TN):
    m, k = x2d.shape
    n = b.shape[1]
    assert m % tm == 0 and n % tn == 0 and seq % tm == 0
    out_shape = [jax.ShapeDtypeStruct((m, n), F32)]
    out_specs = [pl.BlockSpec((tm, tn), lambda i, j: (i, j))]
    if want_f32:
        out_shape.append(jax.ShapeDtypeStruct((m, k), F32))
        out_specs.append(pl.BlockSpec((tm, k), lambda i, j: (i, 0)))
    return pl.pallas_call(
        functools.partial(_mm_norm_body, want_f32=want_f32),
        grid=(m // tm, n // tn),
        in_specs=[
            pl.BlockSpec((tm, k), lambda i, j: (i, 0)),
            pl.BlockSpec((1, k), lambda i, j: (0, 0)),
            pl.BlockSpec((None, 1, k), lambda i, j: ((i * tm // seq) * 6 + shift_idx, 0, 0)),
            pl.BlockSpec((None, 1, k), lambda i, j: ((i * tm // seq) * 6 + scale_idx, 0, 0)),
            pl.BlockSpec((k, tn), lambda i, j: (0, j)),
            pl.BlockSpec((1, tn), lambda i, j: (0, j)),
        ],
        out_specs=out_specs,
        out_shape=out_shape,
        scratch_shapes=[pltpu.VMEM((tm, k), BF16)],
        compiler_params=_cparams(("parallel", "arbitrary")),
        name="mm_norm",
    )(x2d, g, mod6, mod6, b, bias)


def _head_norm_body(q_ref, sl_ref, wn_ref, gq_ref, gs_ref, gw_ref, qn_ref, kv_ref):
    def rms(x, g):
        y = x * lax.rsqrt(jnp.mean(x * x, axis=-1, keepdims=True) + EPS)
        return y * g

    qscale = HEAD_DIM ** -0.5
    for c in range(N_HEADS):
        sl = slice(c * HEAD_DIM, (c + 1) * HEAD_DIM)
        qn_ref[:, sl] = (rms(q_ref[:, sl], gq_ref[...]) * qscale).astype(BF16)
    for g in range(N_KV_GROUPS):
        sl = slice(g * HEAD_DIM, (g + 1) * HEAD_DIM)
        sv = slice(KV_COLS + g * HEAD_DIM, KV_COLS + (g + 1) * HEAD_DIM)
        kv_ref[:, sl] = rms(sl_ref[:, sl], gs_ref[...]).astype(BF16)
        kv_ref[:, sv] = sl_ref[:, sv].astype(BF16)
        kv_ref[:, slice(2 * KV_COLS + g * HEAD_DIM, 2 * KV_COLS + (g + 1) * HEAD_DIM)] = (
            rms(wn_ref[:, sl], gw_ref[...]).astype(BF16))
        kv_ref[:, slice(3 * KV_COLS + g * HEAD_DIM, 3 * KV_COLS + (g + 1) * HEAD_DIM)] = (
            wn_ref[:, sv].astype(BF16))


def _head_norm(z, col_q, col_slc, col_win, gq, gs, gw, tr=ROW_TILE):
    n = z.shape[0]
    w2 = 2 * KV_COLS
    assert n % tr == 0 and col_q % Q_COLS == 0 and col_slc % w2 == 0 and col_win % w2 == 0
    return pl.pallas_call(
        _head_norm_body,
        grid=(n // tr,),
        in_specs=[
            pl.BlockSpec((tr, Q_COLS), lambda i: (i, col_q // Q_COLS)),
            pl.BlockSpec((tr, w2), lambda i: (i, col_slc // w2)),
            pl.BlockSpec((tr, w2), lambda i: (i, col_win // w2)),
            pl.BlockSpec((1, HEAD_DIM), lambda i: (0, 0)),
            pl.BlockSpec((1, HEAD_DIM), lambda i: (0, 0)),
            pl.BlockSpec((1, HEAD_DIM), lambda i: (0, 0)),
        ],
        out_specs=[
            pl.BlockSpec((tr, Q_COLS), lambda i: (i, 0)),
            pl.BlockSpec((tr, 4 * KV_COLS), lambda i: (i, 0)),
        ],
        out_shape=[
            jax.ShapeDtypeStruct((n, Q_COLS), BF16),
            jax.ShapeDtypeStruct((n, 4 * KV_COLS), BF16),
        ],
        compiler_params=_cparams(("parallel",)),
        name="head_norm",
    )(z, z, z, gq, gs, gw)


def _compress_body(z_ref, pa_ref, pb_ref, wt_ref, wb_ref, w2_ref, g_ref, o_ref, *, normalize):
    n_chunk = z_ref.shape[0] // STRIDE_CMP
    top = jnp.zeros((n_chunk, HEAD_DIM), F32)
    bot = jnp.zeros((n_chunk, HEAD_DIM), F32)
    for l in range(STRIDE_CMP):
        rows = z_ref[pl.ds(l, n_chunk, stride=STRIDE_CMP), :]
        wl = slice(l * HEAD_DIM, (l + 1) * HEAD_DIM)
        top = top + jnp.dot((rows + pa_ref[l:l + 1, :]).astype(BF16), wt_ref[wl, :].astype(BF16),
                            preferred_element_type=F32)
        bot = bot + jnp.dot((rows + pb_ref[l:l + 1, :]).astype(BF16), wb_ref[wl, :].astype(BF16),
                            preferred_element_type=F32)
    pre = top + pltpu.roll(bot, n_chunk - 1, 0)
    out = jnp.dot(_gelu_tanh(pre).astype(BF16), w2_ref[...].astype(BF16), preferred_element_type=F32)
    if normalize:
        out = out * lax.rsqrt(jnp.mean(out * out, axis=-1, keepdims=True) + EPS) * g_ref[...]
    o_ref[...] = out.astype(BF16)


def _compress(z, col, pos, w1, w2, gain, normalize, batch, seq):
    g = N_KV_GROUPS
    n_chunk = seq // STRIDE_CMP
    half = STRIDE_CMP * HEAD_DIM
    assert col % HEAD_DIM == 0 and seq % STRIDE_CMP == 0
    full = lambda *shape: pl.BlockSpec(shape, lambda bi, gi: (0,) * len(shape))
    return pl.pallas_call(
        functools.partial(_compress_body, normalize=normalize),
        grid=(batch, g),
        in_specs=[
            pl.BlockSpec((seq, HEAD_DIM), lambda bi, gi: (bi, col // HEAD_DIM + gi)),
            full(STRIDE_CMP, HEAD_DIM), full(STRIDE_CMP, HEAD_DIM),
            full(half, HEAD_DIM), full(half, HEAD_DIM), full(HEAD_DIM, HEAD_DIM), full(1, HEAD_DIM),
        ],
        out_specs=pl.BlockSpec((None, None, n_chunk, HEAD_DIM), lambda bi, gi: (bi, gi, 0, 0)),
        out_shape=jax.ShapeDtypeStruct((batch, g, n_chunk, HEAD_DIM), BF16),
        compiler_params=_cparams(("parallel", "parallel")),
        name="compress",
    )(z, pos[:STRIDE_CMP], pos[STRIDE_CMP:], w1[:half], w1[half:], w2, gain)


def _nsa_body(q_ref, kc_ref, vc_ref, ks_ref, vs_ref, kw_ref, vw_ref, g_ref, o_ref,
              qs_ref, sc_ref, oc_ref, *, tq, tk, seq):
    n_cmp = (seq - L_CMP) // STRIDE_CMP + 1
    n_blk = seq // L_SEL
    n_sel = min(N_SEL, n_blk)
    ncp = kc_ref.shape[0]
    hpg = HEADS_PER_GROUP
    wk = WINDOW + tq
    nt_dims = (((1,), (1,)), ((), ()))
    masked = -NEG_BIG
    m_floor = -0.1 * NEG_BIG

    t0 = pl.program_id(2) * tq
    t_col = t0 + lax.broadcasted_iota(I32, (tq, 1), 0)

    for h in range(hpg):
        qs_ref[h * tq:(h + 1) * tq, :] = q_ref[:, h * HEAD_DIM:(h + 1) * HEAD_DIM]
    q_all = qs_ref[...]

    def per_head(x):
        return jnp.concatenate([x] * hpg, axis=0)

    def softmax_terms(s):
        m = jnp.maximum(jnp.max(s, axis=-1, keepdims=True), m_floor)
        e = jnp.exp(s - m)
        return e, jnp.sum(e, axis=-1, keepdims=True)

    gates = _sigmoid(g_ref[...])
    gate_col = lambda c: jnp.concatenate([gates[:, 3 * h + c:3 * h + c + 1] for h in range(hpg)], axis=0)

    n_row = lax.broadcasted_iota(I32, (1, ncp), 1)
    vis = (n_row * STRIDE_CMP + (L_CMP - 1) <= t_col) & (n_row < n_cmp)
    s = lax.dot_general(q_all, kc_ref[...], nt_dims, preferred_element_type=F32)
    e, l = softmax_terms(s + per_head(jnp.where(vis, 0.0, masked)))
    p = e / jnp.maximum(l, 1e-30)
    oc_ref[...] = gate_col(0) * jnp.dot(p.astype(BF16), vc_ref[...], preferred_element_type=F32)
    psum = p[0:tq]
    for h in range(1, hpg):
        psum = psum + p[h * tq:(h + 1) * tq]

    w0 = pl.multiple_of(jnp.minimum(jnp.maximum(t0 - WINDOW, 0), seq - wk), tq)
    dist = t_col - (w0 + lax.broadcasted_iota(I32, (1, wk), 1))
    bias_w = jnp.where(dist >= 0, jnp.where(dist < WINDOW, 0.0, masked), masked)
    s = lax.dot_general(q_all, kw_ref[pl.ds(w0, wk), :], nt_dims, preferred_element_type=F32)
    e, l = softmax_terms(s + per_head(bias_w))
    o_win = jnp.dot(e.astype(BF16), vw_ref[pl.ds(w0, wk), :], preferred_element_type=F32)
    oc_ref[...] = oc_ref[...] + gate_col(2) * (o_win / l)

    j_row = lax.broadcasted_iota(I32, (1, LANES), 1)
    n_colv = lax.broadcasted_iota(I32, (ncp, 1), 0)
    overlap = ((n_colv * STRIDE_CMP < j_row * L_SEL + L_SEL)
               & (n_colv * STRIDE_CMP + L_CMP > j_row * L_SEL)
               & (n_colv < n_cmp) & (j_row < n_blk))
    overlap = jnp.where(overlap, 1.0, 0.0).astype(BF16)
    p_hi = psum.astype(BF16)
    r1 = psum - p_hi.astype(F32)
    p_mid = r1.astype(BF16)
    p_lo = (r1 - p_mid.astype(F32)).astype(BF16)
    imp = (jnp.dot(p_hi, overlap, preferred_element_type=F32)
           + jnp.dot(p_mid, overlap, preferred_element_type=F32)
           + jnp.dot(p_lo, overlap, preferred_element_type=F32))

    sel_shift = L_SEL.bit_length() - 1
    cur = jnp.right_shift(t_col, sel_shift)
    forced = (j_row == 0) | (j_row == cur) | (j_row == cur - 1)
    score = jnp.where(forced, NEG_BIG, jnp.where(j_row * L_SEL <= t_col, imp, -NEG_BIG))
    sc_ref[...] = jnp.transpose(score)
    n_grp = n_blk // SUBLANES
    groups = [sc_ref[g * SUBLANES:(g + 1) * SUBLANES, :] for g in range(n_grp)]
    j_sub = lax.broadcasted_iota(I32, (SUBLANES, tq), 0)
    ranks = [jnp.zeros((SUBLANES, tq), F32) for _ in range(n_grp)]
    for i in range(n_blk):
        rb = jnp.broadcast_to(sc_ref[i:i + 1, :], (SUBLANES, tq))
        for g in range(n_grp):
            lo = g * SUBLANES
            if lo > i:
                beats = jnp.where(rb >= groups[g], 1.0, 0.0)
            elif lo + SUBLANES - 1 <= i:
                beats = jnp.where(rb > groups[g], 1.0, 0.0)
            else:
                beats = jnp.where(j_sub + lo > i, jnp.where(rb >= groups[g], 1.0, 0.0),
                                  jnp.where(rb > groups[g], 1.0, 0.0))
            ranks[g] = ranks[g] + beats
    sel_t = jnp.concatenate([jnp.where(r < n_sel, 1.0, 0.0) for r in ranks]
                            + [jnp.zeros((LANES - n_blk, tq), F32)], axis=0)
    sel = jnp.transpose(sel_t).astype(BF16)

    c_row = lax.broadcasted_iota(I32, (1, tk), 1)
    j_colv = lax.broadcasted_iota(I32, (LANES, 1), 0)
    ones_cols = jnp.ones((tk, HEAD_DIM), BF16)
    rows = hpg * tq

    def slc_block(jb, state):
        m_old, l_old, acc_old = state
        k0 = jb * tk
        kpos = k0 + c_row
        expand = jnp.where(j_colv == jnp.right_shift(kpos, sel_shift), 1.0, 0.0).astype(BF16)
        chosen = jnp.dot(sel, expand, preferred_element_type=F32)
        bias = jnp.where(kpos <= t_col, jnp.where(chosen > 0.5, 0.0, masked), masked)
        s = lax.dot_general(q_all, ks_ref[k0:k0 + tk, :], nt_dims, preferred_element_type=F32)
        s = s + per_head(bias)
        m_new = jnp.maximum(m_old, jnp.max(s, axis=-1, keepdims=True))
        p = jnp.exp(s - m_new).astype(BF16)
        alpha = jnp.broadcast_to(jnp.exp(m_old - m_new), (rows, HEAD_DIM))
        pv = jnp.dot(p, jnp.concatenate([vs_ref[k0:k0 + tk, :], ones_cols], axis=1), preferred_element_type=F32)
        return m_new, alpha * l_old + pv[:, HEAD_DIM:], alpha * acc_old + pv[:, :HEAD_DIM]

    n_blocks = lax.div(t0 + (tq + tk - 1), tk)
    for nb in range(1, seq // tk + 1):
        @pl.when(n_blocks == nb)
        def _(nb=nb):
            state = (jnp.full((rows, 1), m_floor, F32), jnp.zeros((rows, HEAD_DIM), F32),
                     jnp.zeros((rows, HEAD_DIM), F32))
            for jb in range(nb):
                state = slc_block(jb, state)
            out = oc_ref[...] + gate_col(1) * (state[2] / state[1])
            for h in range(hpg):
                o_ref[:, h * HEAD_DIM:(h + 1) * HEAD_DIM] = out[h * tq:(h + 1) * tq].astype(BF16)


def _nsa(qn, kvn, kcn, vcm, z, col_gate, batch, seq, tq=NSA_TQ, tk=NSA_TK):
    n = qn.shape[0]
    g = N_KV_GROUPS
    gw = HEADS_PER_GROUP * HEAD_DIM
    nq = seq // tq
    ncp = kcn.shape[2]
    rows = HEADS_PER_GROUP * tq
    assert seq % tq == 0 and seq % tk == 0 and col_gate % LANES == 0
    assert seq // L_SEL <= LANES and (seq // L_SEL) % SUBLANES == 0 and seq >= WINDOW + tq
    kv_spec = lambda part: pl.BlockSpec((seq, HEAD_DIM), lambda b, gi, i: (b, part * g + gi))
    cmp_spec = pl.BlockSpec((None, None, ncp, HEAD_DIM), lambda b, gi, i: (b, gi, 0, 0))
    return pl.pallas_call(
        functools.partial(_nsa_body, tq=tq, tk=tk, seq=seq),
        grid=(batch, g, nq),
        in_specs=[
            pl.BlockSpec((tq, gw), lambda b, gi, i: (b * nq + i, gi)),
            cmp_spec, cmp_spec,
            kv_spec(0), kv_spec(1), kv_spec(2), kv_spec(3),
            pl.BlockSpec((tq, LANES), lambda b, gi, i: (b * nq + i, col_gate // LANES + gi)),
        ],
        out_specs=pl.BlockSpec((tq, gw), lambda b, gi, i: (b * nq + i, gi)),
        out_shape=jax.ShapeDtypeStruct((n, Q_COLS), BF16),
        scratch_shapes=[pltpu.VMEM((rows, HEAD_DIM), BF16), pltpu.VMEM((LANES, tq), F32),
                        pltpu.VMEM((rows, HEAD_DIM), F32)],
        compiler_params=_cparams(("parallel", "parallel", "arbitrary")),
        name="nsa",
    )(qn, kcn, vcm, kvn, kvn, kvn, kvn, z)


def _conv_body(u_ref, v_ref, hu_ref, hv_ref, w_ref, b_ref, lg_ref, lb_ref, o_ref, ubuf, ybuf, *, ts, halo):
    i = pl.program_id(1)
    hal = hu_ref[...] * _sigmoid(hv_ref[...])
    ubuf[0:halo, :] = jnp.where(i > 0, hal, 0.0)
    ubuf[halo:halo + ts, :] = u_ref[...] * _sigmoid(v_ref[...])
    ch = u_ref.shape[1]
    off = halo - (CONV_WIDTH - 1)
    for c in range(ch // LANES):
        sl = slice(c * LANES, (c + 1) * LANES)
        acc = jnp.zeros((ts, LANES), F32)
        for k in range(CONV_WIDTH):
            acc = acc + ubuf[off + k:off + k + ts, sl] * w_ref[k:k + 1, sl]
        ybuf[:, sl] = acc + b_ref[:, sl]
    y = ybuf[...]
    mu = jnp.mean(y, axis=-1, keepdims=True)
    yc = y - mu
    yn = yc * lax.rsqrt(jnp.mean(yc * yc, axis=-1, keepdims=True) + EPS) * lg_ref[...] + lb_ref[...]
    o_ref[...] = (yn * _sigmoid(yn)).astype(BF16)


def _conv_module(z, col_u, col_v, dw_w, dw_b, ln_g, ln_b, batch, seq, ts=CONV_TS, halo=CONV_HALO):
    n = z.shape[0]
    ch = dw_w.shape[1]
    ns = seq // ts
    r = ts // halo
    assert seq % ts == 0 and ts % halo == 0 and col_u % ch == 0 and col_v % ch == 0
    halo_map = lambda col: (lambda b, i: (jnp.maximum((b * ns + i) * r - 1, 0), col // ch))
    full = lambda *shape: pl.BlockSpec(shape, lambda b, i: (0,) * len(shape))
    return pl.pallas_call(
        functools.partial(_conv_body, ts=ts, halo=halo),
        grid=(batch, ns),
        in_specs=[
            pl.BlockSpec((ts, ch), lambda b, i: (b * ns + i, col_u // ch)),
            pl.BlockSpec((ts, ch), lambda b, i: (b * ns + i, col_v // ch)),
            pl.BlockSpec((halo, ch), halo_map(col_u)),
            pl.BlockSpec((halo, ch), halo_map(col_v)),
            full(CONV_WIDTH, ch), full(1, ch), full(1, ch), full(1, ch),
        ],
        out_specs=pl.BlockSpec((ts, ch), lambda b, i: (b * ns + i, 0)),
        out_shape=jax.ShapeDtypeStruct((n, ch), BF16),
        scratch_shapes=[pltpu.VMEM((ts + halo, ch), F32), pltpu.VMEM((ts, ch), F32)],
        compiler_params=_cparams(("parallel", "arbitrary")),
        name="conv_module",
    )(z, z, z, z, dw_w, dw_b, ln_g, ln_b)


def _peer_topk_body(q_ref, sk_ref, e_ref, g_ref, work, v1, i1, v2, i2, cand, candi, ts_buf, te_buf):
    k = PEER_TOPK
    tmn = q_ref.shape[0]
    nt_dims = (((1,), (1,)), ((), ()))

    def extract(src, nrows, val_out, idx_out, payload=None):
        rows = lax.broadcasted_iota(I32, (nrows, tmn), 0).astype(F32)

        def body(r, carry):
            x = src[0:nrows, :]
            m = jnp.max(x, axis=0, keepdims=True)
            pos = jnp.min(jnp.where(x == m, rows, float(nrows)), axis=0, keepdims=True)
            hit = rows == pos
            val_out[pl.ds(r, 1), :] = m
            if payload is None:
                idx_out[pl.ds(r, 1), :] = pos
            else:
                idx_out[pl.ds(r, 1), :] = jnp.max(jnp.where(hit, payload[...], -1.0), axis=0, keepdims=True)
            src[0:nrows, :] = jnp.where(hit, -jnp.inf, x)
            return carry

        lax.fori_loop(0, k, body, 0)

    for p, (vo, io) in enumerate(((v1, i1), (v2, i2))):
        qh = q_ref[:, p * PEER_HALF:(p + 1) * PEER_HALF].astype(BF16)
        work[0:PEER_NKEYS, :] = lax.dot_general(sk_ref[p].astype(BF16), qh, nt_dims,
                                                preferred_element_type=F32)
        extract(work, PEER_NKEYS, vo, io)

    off = 0
    for a in range(k):
        w = k // (a + 1)
        cand[off:off + w, :] = v1[a:a + 1, :] + v2[0:w, :]
        candi[off:off + w, :] = i1[a:a + 1, :] * float(PEER_NKEYS) + i2[0:w, :]
        off += w
    n_rows = cand.shape[0]
    cand[off:n_rows, :] = jnp.full((n_rows - off, tmn), -jnp.inf, F32)
    candi[off:n_rows, :] = jnp.zeros((n_rows - off, tmn), F32)
    extract(cand, n_rows, ts_buf, te_buf, payload=candi)

    ts = ts_buf[...]
    ex = jnp.exp(ts - jnp.max(ts, axis=0, keepdims=True))
    g_ref[...] = ex / jnp.sum(ex, axis=0, keepdims=True)
    e_ref[...] = te_buf[...].astype(I32)


def _peer_topk(qp, sub_keys, tmn=PEER_TM):
    n = qp.shape[0]
    k = PEER_TOPK
    qd = 2 * PEER_HALF
    sk = sub_keys.reshape(PEER_HEADS, 2, PEER_NKEYS, PEER_HALF)
    assert n % tmn == 0
    rows = PEER_HEADS * k
    n_cand = sum(k // (a + 1) for a in range(k))
    cand_rows = -(-n_cand // SUBLANES) * SUBLANES
    small_f = pltpu.VMEM((k, tmn), F32)
    return pl.pallas_call(
        _peer_topk_body,
        grid=(n // tmn, PEER_HEADS),
        in_specs=[
            pl.BlockSpec((tmn, qd), lambda i, h: (i, h)),
            pl.BlockSpec((None, 2, PEER_NKEYS, PEER_HALF), lambda i, h: (h, 0, 0, 0)),
        ],
        out_specs=[
            pl.BlockSpec((k, tmn), lambda i, h: (h, i)),
            pl.BlockSpec((k, tmn), lambda i, h: (h, i)),
        ],
        out_shape=[
            jax.ShapeDtypeStruct((rows, n), I32),
            jax.ShapeDtypeStruct((rows, n), F32),
        ],
        scratch_shapes=[
            pltpu.VMEM((PEER_NKEYS, tmn), F32),
            small_f, small_f, small_f, small_f,
            pltpu.VMEM((cand_rows, tmn), F32), pltpu.VMEM((cand_rows, tmn), F32),
            small_f, small_f,
        ],
        compiler_params=_cparams(("parallel", "arbitrary")),
        name="peer_topk",
    )(qp, sk)


def _uv_pack_body(u_ref, v_ref, o_ref):
    te, d = u_ref.shape
    half = d // LANES
    o_ref[:, 0:half, :] = u_ref[...].reshape(te, half, LANES).astype(BF16)
    o_ref[:, half:2 * half, :] = v_ref[...].reshape(te, half, LANES).astype(BF16)


def _uv_pack(u, v, te=UV_TE):
    ne, d = u.shape
    half = d // LANES
    assert ne % te == 0 and d % LANES == 0
    return pl.pallas_call(
        _uv_pack_body,
        grid=(ne // te,),
        in_specs=[pl.BlockSpec((te, d), lambda i: (i, 0))] * 2,
        out_specs=pl.BlockSpec((te, 2 * half, LANES), lambda i: (i, 0, 0)),
        out_shape=jax.ShapeDtypeStruct((ne, 2 * half, LANES), BF16),
        compiler_params=_cparams(("parallel",)),
        name="uv_pack",
    )(u, v)


def _peer_mix_body(idx_cur, idx_nxt, uv_hbm, h_ref, gate_ref, x_ref, g2_ref, o_ref,
                   buf_a, buf_b, cbuf, sem, *, tb, ne):
    i = pl.program_id(0)
    nb = pl.num_programs(0)
    half = h_ref.shape[1]
    bufs = (buf_a, buf_b)

    def row_copy(e_idx, which, row):
        return pltpu.make_async_copy(uv_hbm.at[e_idx], bufs[which].at[row], sem.at[which])

    def block_wait(which):
        pltpu.make_async_copy(uv_hbm.at[pl.ds(0, tb * ne)], bufs[which], sem.at[which]).wait()

    def issue(idx_ref, row0, which, t):
        for e in range(ne):
            row_copy(idx_ref[row0 + t, e], which, t * ne + e).start(priority=e % 2)

    @pl.when(i == 0)
    def _():
        def issue_first(t, carry):
            issue(idx_cur, 0, 0, t)
            return carry
        lax.fori_loop(0, tb, issue_first, 0)

    lane = lax.broadcasted_iota(I32, (SUBLANES, LANES), 1)

    def phase(which, tok0, idx_ref, idx_row0):
        src = bufs[which]

        def u_side(t):
            h = h_ref[tok0 + t]
            acc = jnp.zeros((SUBLANES, LANES), F32)
            for e in range(ne):
                prod = src[t * ne + e, 0:half, :].astype(F32) * h
                part = prod[0:SUBLANES]
                for s in range(1, half // SUBLANES):
                    part = part + prod[s * SUBLANES:(s + 1) * SUBLANES]
                acc = jnp.where(lane == e, jnp.sum(part, axis=-1, keepdims=True), acc)
            a = jnp.sum(acc, axis=0, keepdims=True)
            coef = _gelu_tanh(a) * gate_ref[pl.ds(tok0 + t, 1), :]
            cbuf[t % 2] = jnp.transpose(jnp.broadcast_to(coef, (ne, ne)))

        def v_side(t):
            out = jnp.zeros((half, LANES), F32)
            for e in range(ne):
                out = out + cbuf[t % 2, e:e + 1, :] * src[t * ne + e, half:2 * half, :].astype(F32)
            o_ref[tok0 + t] = x_ref[tok0 + t] + g2_ref[...] * out

        block_wait(which)
        u_side(0)
        for t in range(tb - 1):
            issue(idx_ref, idx_row0, 1 - which, t)
            v_side(t)
            u_side(t + 1)
        issue(idx_ref, idx_row0, 1 - which, tb - 1)
        v_side(tb - 1)

    phase(0, 0, idx_cur, tb)
    phase(1, tb, idx_nxt, 0)

    @pl.when(i == nb - 1)
    def _():
        block_wait(0)


def _peer_mix(idx, gate, uv, h2, x1, mod6, gate_idx, seq, tb=PEER_TB):
    n, ne = idx.shape
    d = h2.shape[1]
    half = d // LANES
    tb2 = 2 * tb
    nb = n // tb2
    assert n % tb2 == 0 and seq % tb2 == 0 and ne == LANES
    h3 = h2.reshape(n, half, LANES)
    x3 = x1.reshape(n, half, LANES)
    m3 = mod6.reshape(mod6.shape[0], half, LANES)
    rows = pltpu.VMEM((tb * ne, 2 * half, LANES), BF16)
    out = pl.pallas_call(
        functools.partial(_peer_mix_body, tb=tb, ne=ne),
        grid=(nb,),
        in_specs=[
            pl.BlockSpec((tb2, ne), lambda i: (i, 0), memory_space=pltpu.SMEM),
            pl.BlockSpec((tb2, ne), lambda i: (jnp.minimum(i + 1, nb - 1), 0), memory_space=pltpu.SMEM),
            pl.BlockSpec(memory_space=pl.ANY),
            pl.BlockSpec((tb2, half, LANES), lambda i: (i, 0, 0)),
            pl.BlockSpec((tb2, ne), lambda i: (i, 0)),
            pl.BlockSpec((tb2, half, LANES), lambda i: (i, 0, 0)),
            pl.BlockSpec((None, half, LANES), lambda i: ((i * tb2 // seq) * 6 + gate_idx, 0, 0)),
        ],
        out_specs=pl.BlockSpec((tb2, half, LANES), lambda i: (i, 0, 0)),
        out_shape=jax.ShapeDtypeStruct((n, half, LANES), F32),
        scratch_shapes=[rows, rows, pltpu.VMEM((2, ne, ne), F32), pltpu.SemaphoreType.DMA((2,))],
        compiler_params=_cparams(("arbitrary",)),
        name="peer_mix",
    )(idx, idx, uv, h3, gate, x3, m3)
    return out.reshape(n, d)


def _layer(x, c, norm1_g, norm2_g, w_ada, b_ada, w_in, b_in, q_norm_g, k_norm_g,
           cmp_pos_k, cmp_pos_v, cmp_k_w1, cmp_k_w2, cmp_v_w1, cmp_v_w2, w_nsa_out,
           conv_dw_w, conv_dw_b, conv_ln_g, conv_ln_b, conv_pw_w, conv_pw_b, w_out,
           peer_w_q, peer_sub_keys, peer_u, peer_v):
    b, s, d = x.shape
    n = b * s
    g = N_KV_GROUPS
    conv_ch = conv_dw_w.shape[1]
    x2 = x.reshape(n, d)

    rows = max(SUBLANES, b)
    sc = jnp.zeros((rows, d), F32).at[:b].set(c * _sigmoid(c))
    mod = _matmul_bias(sc, w_ada, b_ada.reshape(1, -1), tm=rows)[:b]
    mod6 = mod.reshape(b * 6, 1, d)

    sizes = (Q_COLS,) + (KV_COLS,) * 6 + (3 * N_HEADS, 2 * conv_ch, 2 * d)
    offs = [0]
    for sz in sizes:
        offs.append(offs[-1] + sz)
    part = lambda arr, k: arr[..., offs[k]:offs[k + 1]]
    per_group = 3 * HEADS_PER_GROUP

    def gate_cols(arr):
        a = part(arr, 7).reshape(arr.shape[:-1] + (g, per_group))
        a = jnp.pad(a, [(0, 0)] * (a.ndim - 1) + [(0, LANES - per_group)])
        return a.reshape(arr.shape[:-1] + (GATE_PAD,))

    order = [0, 8, 9, 1, 2, 3, 4, 5, 6]
    w_cat = jnp.concatenate([part(w_in, k) for k in order] + [gate_cols(w_in)], axis=-1).astype(BF16)
    b_cat = jnp.concatenate([part(b_in, k) for k in order] + [gate_cols(b_in)], axis=-1).reshape(1, -1)
    col_q = 0
    col_glu = Q_COLS
    col_merge = col_glu + 2 * conv_ch
    col_cmp = col_merge + 2 * d
    col_slc = col_cmp + 2 * KV_COLS
    col_win = col_slc + 2 * KV_COLS
    col_gate = col_win + 2 * KV_COLS

    z, = _matmul_norm(x2, norm1_g.reshape(1, d), mod6, 0, 1, s, w_cat, b_cat, want_f32=False, tm=MM_TM)

    qn, kvn = _head_norm(z, col_q, col_slc, col_win, q_norm_g.reshape(1, -1),
                         k_norm_g[1].reshape(1, -1), k_norm_g[2].reshape(1, -1))
    kcn = _compress(z, col_cmp, cmp_pos_k, cmp_k_w1, cmp_k_w2, k_norm_g[0].reshape(1, -1), True, b, s)
    vcm = _compress(z, col_cmp + KV_COLS, cmp_pos_v, cmp_v_w1, cmp_v_w2, k_norm_g[0].reshape(1, -1), False, b, s)

    attn = _nsa(qn, kvn, kcn, vcm, z, col_gate, b, s)
    y_attn = _matmul_bias(attn, w_nsa_out, jnp.zeros((1, d), F32))

    yc = _conv_module(z, col_glu, col_glu + conv_ch, conv_dw_w, conv_dw_b.reshape(1, -1),
                      conv_ln_g.reshape(1, -1), conv_ln_b.reshape(1, -1), b, s)
    merged = _matmul_merge(yc, conv_pw_w, conv_pw_b.reshape(1, -1), z, col_merge, col_merge + d, y_attn)
    x1 = _matmul_resid(merged, w_out, x2, mod6, 2, s)

    qp, h2f = _matmul_norm(x1, norm2_g.reshape(1, d), mod6, 3, 4, s, peer_w_q,
                           jnp.zeros((1, peer_w_q.shape[1]), F32), want_f32=True, tm=MM_TM // 2)
    e_t, g_t = _peer_topk(qp, peer_sub_keys)
    idx = e_t.T
    gate = g_t.T
    uv = _uv_pack(peer_u, peer_v)
    out = _peer_mix(idx, gate, uv, h2f, x1, mod6, 5, s)
    return out.reshape(b, s, d)


def kernel(x, c, norm1_g, norm2_g, w_ada, b_ada, w_in, b_in, q_norm_g, k_norm_g, cmp_pos_k, cmp_pos_v,
           cmp_k_w1, cmp_k_w2, cmp_v_w1, cmp_v_w2, w_nsa_out, conv_dw_w, conv_dw_b, conv_ln_g, conv_ln_b,
           conv_pw_w, conv_pw_b, w_out, peer_w_q, peer_sub_keys, peer_u, peer_v):
    depth = norm1_g.shape[0]
    for l in range(depth):
        x = _layer(x, c, norm1_g[l], norm2_g[l], w_ada[l], b_ada[l], w_in[l], b_in[l],
                   q_norm_g[l], k_norm_g[l], cmp_pos_k[l], cmp_pos_v[l],
                   cmp_k_w1[l], cmp_k_w2[l], cmp_v_w1[l], cmp_v_w2[l], w_nsa_out[l],
                   conv_dw_w[l], conv_dw_b[l], conv_ln_g[l], conv_ln_b[l],
                   conv_pw_w[l], conv_pw_b[l], w_out[l],
                   peer_w_q[l], peer_sub_keys[l], peer_u[l], peer_v[l])
    return x
```

```python
import functools
import math

import jax
import jax.numpy as jnp
from jax import lax
from jax.experimental import pallas as pl
from jax.experimental.pallas import tpu as pltpu

F32 = jnp.float32
BF16 = jnp.bfloat16
I32 = jnp.int32

N_HEADS = 16
N_KV_GROUPS = 4
HEADS_PER_GROUP = N_HEADS // N_KV_GROUPS
HEAD_DIM = 128
L_CMP = 32
STRIDE_CMP = 16
L_SEL = 64
N_SEL = 16
WINDOW = 512
CONV_WIDTH = 31
PEER_HEADS = 8
PEER_NKEYS = 128
PEER_HALF = 128
PEER_TOPK = 16
EPS = 1e-6
NEG_BIG = 1e30

LANES = 128
SUBLANES = 8
VMEM_LIMIT = 48 * 1024 * 1024

MM_TM = 1024
MM_TN = 512
ROW_TILE = 256
NSA_TQ = 256
NSA_TK = 1024
CONV_TS = 256
CONV_HALO = 32
PEER_TM = 1024
PEER_TB = 8
UV_TE = 512

Q_COLS = N_HEADS * HEAD_DIM
KV_COLS = N_KV_GROUPS * HEAD_DIM
GATE_PAD = N_KV_GROUPS * LANES


def _cparams(sem):
    return pltpu.CompilerParams(dimension_semantics=sem, vmem_limit_bytes=VMEM_LIMIT)


def _gelu_tanh(x):
    c = math.sqrt(2.0 / math.pi)
    return x * (0.5 * (1.0 + jnp.tanh(c * (x + 0.044715 * (x * x * x)))))


def _sigmoid(x):
    return 1.0 / (1.0 + jnp.exp(-x))


def _mm_bias_body(a_ref, b_ref, bias_ref, o_ref):
    acc = jnp.dot(a_ref[...].astype(BF16), b_ref[...].astype(BF16), preferred_element_type=F32)
    o_ref[...] = (acc + bias_ref[...]).astype(o_ref.dtype)


def _matmul_bias(a, b, bias, out_dtype=F32, tm=MM_TM, tn=MM_TN):
    m, k = a.shape
    n = b.shape[1]
    tm = min(tm, m)
    tn = min(tn, n)
    assert m % tm == 0 and n % tn == 0
    return pl.pallas_call(
        _mm_bias_body,
        grid=(m // tm, n // tn),
        in_specs=[
            pl.BlockSpec((tm, k), lambda i, j: (i, 0)),
            pl.BlockSpec((k, tn), lambda i, j: (0, j)),
            pl.BlockSpec((1, tn), lambda i, j: (0, j)),
        ],
        out_specs=pl.BlockSpec((tm, tn), lambda i, j: (i, j)),
        out_shape=jax.ShapeDtypeStruct((m, n), out_dtype),
        compiler_params=_cparams(("parallel", "arbitrary")),
        name="mm_bias",
    )(a, b, bias)


def _mm_merge_body(a_ref, b_ref, bias_ref, g0_ref, g1_ref, ya_ref, o_ref):
    acc = jnp.dot(a_ref[...], b_ref[...].astype(BF16), preferred_element_type=F32) + bias_ref[...]
    o_ref[...] = (_sigmoid(g0_ref[...]) * ya_ref[...] + _sigmoid(g1_ref[...]) * acc).astype(o_ref.dtype)


def _matmul_merge(a, b, bias, z, g0_col, g1_col, y_attn, tm=MM_TM, tn=MM_TN):
    m, k = a.shape
    n = b.shape[1]
    assert m % tm == 0 and n % tn == 0 and g0_col % tn == 0 and g1_col % tn == 0
    return pl.pallas_call(
        _mm_merge_body,
        grid=(m // tm, n // tn),
        in_specs=[
            pl.BlockSpec((tm, k), lambda i, j: (i, 0)),
            pl.BlockSpec((k, tn), lambda i, j: (0, j)),
            pl.BlockSpec((1, tn), lambda i, j: (0, j)),
            pl.BlockSpec((tm, tn), lambda i, j: (i, g0_col // tn + j)),
            pl.BlockSpec((tm, tn), lambda i, j: (i, g1_col // tn + j)),
            pl.BlockSpec((tm, tn), lambda i, j: (i, j)),
        ],
        out_specs=pl.BlockSpec((tm, tn), lambda i, j: (i, j)),
        out_shape=jax.ShapeDtypeStruct((m, n), BF16),
        compiler_params=_cparams(("parallel", "arbitrary")),
        name="mm_merge",
    )(a, b, bias, z, z, y_attn)


def _mm_resid_body(a_ref, b_ref, res_ref, gate_ref, o_ref):
    acc = jnp.dot(a_ref[...], b_ref[...].astype(BF16), preferred_element_type=F32)
    o_ref[...] = res_ref[...] + gate_ref[...] * acc


def _matmul_resid(a, b, res, mod6, gate_idx, seq, tm=MM_TM, tn=MM_TN):
    m, k = a.shape
    n = b.shape[1]
    assert m % tm == 0 and n % tn == 0 and seq % tm == 0
    return pl.pallas_call(
        _mm_resid_body,
        grid=(m // tm, n // tn),
        in_specs=[
            pl.BlockSpec((tm, k), lambda i, j: (i, 0)),
            pl.BlockSpec((k, tn), lambda i, j: (0, j)),
            pl.BlockSpec((tm, tn), lambda i, j: (i, j)),
            pl.BlockSpec((None, 1, tn), lambda i, j: ((i * tm // seq) * 6 + gate_idx, 0, j)),
        ],
        out_specs=pl.BlockSpec((tm, tn), lambda i, j: (i, j)),
        out_shape=jax.ShapeDtypeStruct((m, n), F32),
        compiler_params=_cparams(("parallel", "arbitrary")),
        name="mm_resid",
    )(a, b, res, mod6)


def _mm_norm_body(x_ref, g_ref, shift_ref, scale_ref, b_ref, bias_ref, o_ref, *rest, want_f32):
    hb_ref = rest[-1]

    @pl.when(pl.program_id(1) == 0)
    def _():
        x = x_ref[...]
        y = x * lax.rsqrt(jnp.mean(x * x, axis=-1, keepdims=True) + EPS)
        h = (y * g_ref[...]) * (1.0 + scale_ref[...]) + shift_ref[...]
        hb_ref[...] = h.astype(BF16)
        if want_f32:
            rest[0][...] = h

    acc = jnp.dot(hb_ref[...], b_ref[...].astype(BF16), preferred_element_type=F32)
    o_ref[...] = acc + bias_ref[...]


def _matmul_norm(x2d, g, mod6, shift_idx, scale_idx, seq, b, bias, want_f32, tm, tn=MM_TN):
    m, k = x2d.shape
    n = b.shape[1]
    assert m % tm == 0 and n % tn == 0 and seq % tm == 0
    out_shape = [jax.ShapeDtypeStruct((m, n), F32)]
    out_specs = [pl.BlockSpec((tm, tn), lambda i, j: (i, j))]
    if want_f32:
        out_shape.append(jax.ShapeDtypeStruct((m, k), F32))
        out_specs.append(pl.BlockSpec((tm, k), lambda i, j: (i, 0)))
    return pl.pallas_call(
        functools.partial(_mm_norm_body, want_f32=want_f32),
        grid=(m // tm, n // tn),
        in_specs=[
            pl.BlockSpec((tm, k), lambda i, j: (i, 0)),
            pl.BlockSpec((1, k), lambda i, j: (0, 0)),
            pl.BlockSpec((None, 1, k), lambda i, j: ((i * tm // seq) * 6 + shift_idx, 0, 0)),
            pl.BlockSpec((None, 1, k), lambda i, j: ((i * tm // seq) * 6 + scale_idx, 0, 0)),
            pl.BlockSpec((k, tn), lambda i, j: (0, j)),
            pl.BlockSpec((1, tn), lambda i, j: (0, j)),
        ],
        out_specs=out_specs,
        out_shape=out_shape,
        scratch_shapes=[pltpu.VMEM((tm, k), BF16)],
        compiler_params=_cparams(("parallel", "arbitrary")),
        name="mm_norm",
    )(x2d, g, mod6, mod6, b, bias)


def _head_norm_body(q_ref, sl_ref, wn_ref, gq_ref, gs_ref, gw_ref, qn_ref, kv_ref):
    def rms(x, g):
        y = x * lax.rsqrt(jnp.mean(x * x, axis=-1, keepdims=True) + EPS)
        return y * g

    qscale = HEAD_DIM ** -0.5
    for c in range(N_HEADS):
        sl = slice(c * HEAD_DIM, (c + 1) * HEAD_DIM)
        qn_ref[:, sl] = (rms(q_ref[:, sl], gq_ref[...]) * qscale).astype(BF16)
    for g in range(N_KV_GROUPS):
        sl = slice(g * HEAD_DIM, (g + 1) * HEAD_DIM)
        sv = slice(KV_COLS + g * HEAD_DIM, KV_COLS + (g + 1) * HEAD_DIM)
        kv_ref[:, sl] = rms(sl_ref[:, sl], gs_ref[...]).astype(BF16)
        kv_ref[:, sv] = sl_ref[:, sv].astype(BF16)
        kv_ref[:, slice(2 * KV_COLS + g * HEAD_DIM, 2 * KV_COLS + (g + 1) * HEAD_DIM)] = (
            rms(wn_ref[:, sl], gw_ref[...]).astype(BF16))
        kv_ref[:, slice(3 * KV_COLS + g * HEAD_DIM, 3 * KV_COLS + (g + 1) * HEAD_DIM)] = (
            wn_ref[:, sv].astype(BF16))


def _head_norm(z, col_q, col_slc, col_win, gq, gs, gw, tr=ROW_TILE):
    n = z.shape[0]
    w2 = 2 * KV_COLS
    assert n % tr == 0 and col_q % Q_COLS == 0 and col_slc % w2 == 0 and col_win % w2 == 0
    return pl.pallas_call(
        _head_norm_body,
        grid=(n // tr,),
        in_specs=[
            pl.BlockSpec((tr, Q_COLS), lambda i: (i, col_q // Q_COLS)),
            pl.BlockSpec((tr, w2), lambda i: (i, col_slc // w2)),
            pl.BlockSpec((tr, w2), lambda i: (i, col_win // w2)),
            pl.BlockSpec((1, HEAD_DIM), lambda i: (0, 0)),
            pl.BlockSpec((1, HEAD_DIM), lambda i: (0, 0)),
            pl.BlockSpec((1, HEAD_DIM), lambda i: (0, 0)),
        ],
        out_specs=[
            pl.BlockSpec((tr, Q_COLS), lambda i: (i, 0)),
            pl.BlockSpec((tr, 4 * KV_COLS), lambda i: (i, 0)),
        ],
        out_shape=[
            jax.ShapeDtypeStruct((n, Q_COLS), BF16),
            jax.ShapeDtypeStruct((n, 4 * KV_COLS), BF16),
        ],
        compiler_params=_cparams(("parallel",)),
        name="head_norm",
    )(z, z, z, gq, gs, gw)


def _compress_body(z_ref, pa_ref, pb_ref, wt_ref, wb_ref, w2_ref, g_ref, o_ref, *, normalize):
    n_chunk = z_ref.shape[0] // STRIDE_CMP
    top = jnp.zeros((n_chunk, HEAD_DIM), F32)
    bot = jnp.zeros((n_chunk, HEAD_DIM), F32)
    for l in range(STRIDE_CMP):
        rows = z_ref[pl.ds(l, n_chunk, stride=STRIDE_CMP), :]
        wl = slice(l * HEAD_DIM, (l + 1) * HEAD_DIM)
        top = top + jnp.dot((rows + pa_ref[l:l + 1, :]).astype(BF16), wt_ref[wl, :].astype(BF16),
                            preferred_element_type=F32)
        bot = bot + jnp.dot((rows + pb_ref[l:l + 1, :]).astype(BF16), wb_ref[wl, :].astype(BF16),
                            preferred_element_type=F32)
    pre = top + pltpu.roll(bot, n_chunk - 1, 0)
    out = jnp.dot(_gelu_tanh(pre).astype(BF16), w2_ref[...].astype(BF16), preferred_element_type=F32)
    if normalize:
        out = out * lax.rsqrt(jnp.mean(out * out, axis=-1, keepdims=True) + EPS) * g_ref[...]
    o_ref[...] = out.astype(BF16)


def _compress(z, col, pos, w1, w2, gain, normalize, batch, seq):
    g = N_KV_GROUPS
    n_chunk = seq // STRIDE_CMP
    half = STRIDE_CMP * HEAD_DIM
    assert col % HEAD_DIM == 0 and seq % STRIDE_CMP == 0
    full = lambda *shape: pl.BlockSpec(shape, lambda bi, gi: (0,) * len(shape))
    return pl.pallas_call(
        functools.partial(_compress_body, normalize=normalize),
        grid=(batch, g),
        in_specs=[
            pl.BlockSpec((seq, HEAD_DIM), lambda bi, gi: (bi, col // HEAD_DIM + gi)),
            full(STRIDE_CMP, HEAD_DIM), full(STRIDE_CMP, HEAD_DIM),
            full(half, HEAD_DIM), full(half, HEAD_DIM), full(HEAD_DIM, HEAD_DIM), full(1, HEAD_DIM),
        ],
        out_specs=pl.BlockSpec((None, None, n_chunk, HEAD_DIM), lambda bi, gi: (bi, gi, 0, 0)),
        out_shape=jax.ShapeDtypeStruct((batch, g, n_chunk, HEAD_DIM), BF16),
        compiler_params=_cparams(("parallel", "parallel")),
        name="compress",
    )(z, pos[:STRIDE_CMP], pos[STRIDE_CMP:], w1[:half], w1[half:], w2, gain)


def _nsa_body(q_ref, kc_ref, vc_ref, ks_ref, vs_ref, kw_ref, vw_ref, g_ref, o_ref,
              qs_ref, sc_ref, oc_ref, *, tq, tk, seq):
    n_cmp = (seq - L_CMP) // STRIDE_CMP + 1
    n_blk = seq // L_SEL
    n_sel = min(N_SEL, n_blk)
    ncp = kc_ref.shape[0]
    hpg = HEADS_PER_GROUP
    wk = WINDOW + tq
    nt_dims = (((1,), (1,)), ((), ()))
    masked = -NEG_BIG
    m_floor = -0.1 * NEG_BIG

    t0 = pl.program_id(2) * tq
    t_col = t0 + lax.broadcasted_iota(I32, (tq, 1), 0)

    for h in range(hpg):
        qs_ref[h * tq:(h + 1) * tq, :] = q_ref[:, h * HEAD_DIM:(h + 1) * HEAD_DIM]
    q_all = qs_ref[...]

    def per_head(x):
        return jnp.concatenate([x] * hpg, axis=0)

    def softmax_terms(s):
        m = jnp.maximum(jnp.max(s, axis=-1, keepdims=True), m_floor)
        e = jnp.exp(s - m)
        return e, jnp.sum(e, axis=-1, keepdims=True)

    gates = _sigmoid(g_ref[...])
    gate_col = lambda c: jnp.concatenate([gates[:, 3 * h + c:3 * h + c + 1] for h in range(hpg)], axis=0)

    n_row = lax.broadcasted_iota(I32, (1, ncp), 1)
    vis = (n_row * STRIDE_CMP + (L_CMP - 1) <= t_col) & (n_row < n_cmp)
    s = lax.dot_general(q_all, kc_ref[...], nt_dims, preferred_element_type=F32)
    e, l = softmax_terms(s + per_head(jnp.where(vis, 0.0, masked)))
    p = e / jnp.maximum(l, 1e-30)
    oc_ref[...] = gate_col(0) * jnp.dot(p.astype(BF16), vc_ref[...], preferred_element_type=F32)
    psum = p[0:tq]
    for h in range(1, hpg):
        psum = psum + p[h * tq:(h + 1) * tq]

    w0 = pl.multiple_of(jnp.minimum(jnp.maximum(t0 - WINDOW, 0), seq - wk), tq)
    dist = t_col - (w0 + lax.broadcasted_iota(I32, (1, wk), 1))
    bias_w = jnp.where(dist >= 0, jnp.where(dist < WINDOW, 0.0, masked), masked)
    s = lax.dot_general(q_all, kw_ref[pl.ds(w0, wk), :], nt_dims, preferred_element_type=F32)
    e, l = softmax_terms(s + per_head(bias_w))
    o_win = jnp.dot(e.astype(BF16), vw_ref[pl.ds(w0, wk), :], preferred_element_type=F32)
    oc_ref[...] = oc_ref[...] + gate_col(2) * (o_win / l)

    j_row = lax.broadcasted_iota(I32, (1, LANES), 1)
    n_colv = lax.broadcasted_iota(I32, (ncp, 1), 0)
    overlap = ((n_colv * STRIDE_CMP < j_row * L_SEL + L_SEL)
               & (n_colv * STRIDE_CMP + L_CMP > j_row * L_SEL)
               & (n_colv < n_cmp) & (j_row < n_blk))
    overlap = jnp.where(overlap, 1.0, 0.0).astype(BF16)
    p_hi = psum.astype(BF16)
    r1 = psum - p_hi.astype(F32)
    p_mid = r1.astype(BF16)
    p_lo = (r1 - p_mid.astype(F32)).astype(BF16)
    imp = (jnp.dot(p_hi, overlap, preferred_element_type=F32)
           + jnp.dot(p_mid, overlap, preferred_element_type=F32)
           + jnp.dot(p_lo, overlap, preferred_element_type=F32))

    sel_shift = L_SEL.bit_length() - 1
    cur = jnp.right_shift(t_col, sel_shift)
    forced = (j_row == 0) | (j_row == cur) | (j_row == cur - 1)
    score = jnp.where(forced, NEG_BIG, jnp.where(j_row * L_SEL <= t_col, imp, -NEG_BIG))
    sc_ref[...] = jnp.transpose(score)
    n_grp = n_blk // SUBLANES
    groups = [sc_ref[g * SUBLANES:(g + 1) * SUBLANES, :] for g in range(n_grp)]
    j_sub = lax.broadcasted_iota(I32, (SUBLANES, tq), 0)
    ranks = [jnp.zeros((SUBLANES, tq), F32) for _ in range(n_grp)]
    for i in range(n_blk):
        rb = jnp.broadcast_to(sc_ref[i:i + 1, :], (SUBLANES, tq))
        for g in range(n_grp):
            lo = g * SUBLANES
            if lo > i:
                beats = jnp.where(rb >= groups[g], 1.0, 0.0)
            elif lo + SUBLANES - 1 <= i:
                beats = jnp.where(rb > groups[g], 1.0, 0.0)
            else:
                beats = jnp.where(j_sub + lo > i, jnp.where(rb >= groups[g], 1.0, 0.0),
                                  jnp.where(rb > groups[g], 1.0, 0.0))
            ranks[g] = ranks[g] + beats
    sel_t = jnp.concatenate([jnp.where(r < n_sel, 1.0, 0.0) for r in ranks]
                            + [jnp.zeros((LANES - n_blk, tq), F32)], axis=0)
    sel = jnp.transpose(sel_t).astype(BF16)

    c_row = lax.broadcasted_iota(I32, (1, tk), 1)
    j_colv = lax.broadcasted_iota(I32, (LANES, 1), 0)
    ones_cols = jnp.ones((tk, HEAD_DIM), BF16)
    rows = hpg * tq

    def slc_block(jb, state):
        m_old, l_old, acc_old = state
        k0 = jb * tk
        kpos = k0 + c_row
        expand = jnp.where(j_colv == jnp.right_shift(kpos, sel_shift), 1.0, 0.0).astype(BF16)
        chosen = jnp.dot(sel, expand, preferred_element_type=F32)
        bias = jnp.where(kpos <= t_col, jnp.where(chosen > 0.5, 0.0, masked), masked)
        s = lax.dot_general(q_all, ks_ref[k0:k0 + tk, :], nt_dims, preferred_element_type=F32)
        s = s + per_head(bias)
        m_new = jnp.maximum(m_old, jnp.max(s, axis=-1, keepdims=True))
        p = jnp.exp(s - m_new).astype(BF16)
        alpha = jnp.broadcast_to(jnp.exp(m_old - m_new), (rows, HEAD_DIM))
        pv = jnp.dot(p, jnp.concatenate([vs_ref[k0:k0 + tk, :], ones_cols], axis=1), preferred_element_type=F32)
        return m_new, alpha * l_old + pv[:, HEAD_DIM:], alpha * acc_old + pv[:, :HEAD_DIM]

    n_blocks = lax.div(t0 + (tq + tk - 1), tk)
    for nb in range(1, seq // tk + 1):
        @pl.when(n_blocks == nb)
        def _(nb=nb):
            state = (jnp.full((rows, 1), m_floor, F32), jnp.zeros((rows, HEAD_DIM), F32),
                     jnp.zeros((rows, HEAD_DIM), F32))
            for jb in range(nb):
                state = slc_block(jb, state)
            out = oc_ref[...] + gate_col(1) * (state[2] / state[1])
            for h in range(hpg):
                o_ref[:, h * HEAD_DIM:(h + 1) * HEAD_DIM] = out[h * tq:(h + 1) * tq].astype(BF16)


def _nsa(qn, kvn, kcn, vcm, z, col_gate, batch, seq, tq=NSA_TQ, tk=NSA_TK):
    n = qn.shape[0]
    g = N_KV_GROUPS
    gw = HEADS_PER_GROUP * HEAD_DIM
    nq = seq // tq
    ncp = kcn.shape[2]
    rows = HEADS_PER_GROUP * tq
    assert seq % tq == 0 and seq % tk == 0 and col_gate % LANES == 0
    assert seq // L_SEL <= LANES and (seq // L_SEL) % SUBLANES == 0 and seq >= WINDOW + tq
    kv_spec = lambda part: pl.BlockSpec((seq, HEAD_DIM), lambda b, gi, i: (b, part * g + gi))
    cmp_spec = pl.BlockSpec((None, None, ncp, HEAD_DIM), lambda b, gi, i: (b, gi, 0, 0))
    return pl.pallas_call(
        functools.partial(_nsa_body, tq=tq, tk=tk, seq=seq),
        grid=(batch, g, nq),
        in_specs=[
            pl.BlockSpec((tq, gw), lambda b, gi, i: (b * nq + i, gi)),
            cmp_spec, cmp_spec,
            kv_spec(0), kv_spec(1), kv_spec(2), kv_spec(3),
            pl.BlockSpec((tq, LANES), lambda b, gi, i: (b * nq + i, col_gate // LANES + gi)),
        ],
        out_specs=pl.BlockSpec((tq, gw), lambda b, gi, i: (b * nq + i, gi)),
        out_shape=jax.ShapeDtypeStruct((n, Q_COLS), BF16),
        scratch_shapes=[pltpu.VMEM((rows, HEAD_DIM), BF16), pltpu.VMEM((LANES, tq), F32),
                        pltpu.VMEM((rows, HEAD_DIM), F32)],
        compiler_params=_cparams(("parallel", "parallel", "arbitrary")),
        name="nsa",
    )(qn, kcn, vcm, kvn, kvn, kvn, kvn, z)


def _conv_body(u_ref, v_ref, hu_ref, hv_ref, w_ref, b_ref, lg_ref, lb_ref, o_ref, ubuf, ybuf, *, ts, halo):
    i = pl.program_id(1)
    hal = hu_ref[...] * _sigmoid(hv_ref[...])
    ubuf[0:halo, :] = jnp.where(i > 0, hal, 0.0)
    ubuf[halo:halo + ts, :] = u_ref[...] * _sigmoid(v_ref[...])
    ch = u_ref.shape[1]
    off = halo - (CONV_WIDTH - 1)
    for c in range(ch // LANES):
        sl = slice(c * LANES, (c + 1) * LANES)
        acc = jnp.zeros((ts, LANES), F32)
        for k in range(CONV_WIDTH):
            acc = acc + ubuf[off + k:off + k + ts, sl] * w_ref[k:k + 1, sl]
        ybuf[:, sl] = acc + b_ref[:, sl]
    y = ybuf[...]
    mu = jnp.mean(y, axis=-1, keepdims=True)
    yc = y - mu
    yn = yc * lax.rsqrt(jnp.mean(yc * yc, axis=-1, keepdims=True) + EPS) * lg_ref[...] + lb_ref[...]
    o_ref[...] = (yn * _sigmoid(yn)).astype(BF16)


def _conv_module(z, col_u, col_v, dw_w, dw_b, ln_g, ln_b, batch, seq, ts=CONV_TS, halo=CONV_HALO):
    n = z.shape[0]
    ch = dw_w.shape[1]
    ns = seq // ts
    r = ts // halo
    assert seq % ts == 0 and ts % halo == 0 and col_u % ch == 0 and col_v % ch == 0
    halo_map = lambda col: (lambda b, i: (jnp.maximum((b * ns + i) * r - 1, 0), col // ch))
    full = lambda *shape: pl.BlockSpec(shape, lambda b, i: (0,) * len(shape))
    return pl.pallas_call(
        functools.partial(_conv_body, ts=ts, halo=halo),
        grid=(batch, ns),
        in_specs=[
            pl.BlockSpec((ts, ch), lambda b, i: (b * ns + i, col_u // ch)),
            pl.BlockSpec((ts, ch), lambda b, i: (b * ns + i, col_v // ch)),
            pl.BlockSpec((halo, ch), halo_map(col_u)),
            pl.BlockSpec((halo, ch), halo_map(col_v)),
            full(CONV_WIDTH, ch), full(1, ch), full(1, ch), full(1, ch),
        ],
        out_specs=pl.BlockSpec((ts, ch), lambda b, i: (b * ns + i, 0)),
        out_shape=jax.ShapeDtypeStruct((n, ch), BF16),
        scratch_shapes=[pltpu.VMEM((ts + halo, ch), F32), pltpu.VMEM((ts, ch), F32)],
        compiler_params=_cparams(("parallel", "arbitrary")),
        name="conv_module",
    )(z, z, z, z, dw_w, dw_b, ln_g, ln_b)


def _peer_topk_body(q_ref, sk_ref, e_ref, g_ref, work, v1, i1, v2, i2, cand, candi, ts_buf, te_buf):
    k = PEER_TOPK
    tmn = q_ref.shape[0]
    nt_dims = (((1,), (1,)), ((), ()))

    def extract(src, nrows, val_out, idx_out, payload=None):
        rows = lax.broadcasted_iota(I32, (nrows, tmn), 0).astype(F32)

        def body(r, carry):
            x = src[0:nrows, :]
            m = jnp.max(x, axis=0, keepdims=True)
            pos = jnp.min(jnp.where(x == m, rows, float(nrows)), axis=0, keepdims=True)
            hit = rows == pos
            val_out[pl.ds(r, 1), :] = m
            if payload is None:
                idx_out[pl.ds(r, 1), :] = pos
            else:
                idx_out[pl.ds(r, 1), :] = jnp.max(jnp.where(hit, payload[...], -1.0), axis=0, keepdims=True)
            src[0:nrows, :] = jnp.where(hit, -jnp.inf, x)
            return carry

        lax.fori_loop(0, k, body, 0)

    for p, (vo, io) in enumerate(((v1, i1), (v2, i2))):
        qh = q_ref[:, p * PEER_HALF:(p + 1) * PEER_HALF].astype(BF16)
        work[0:PEER_NKEYS, :] = lax.dot_general(sk_ref[p].astype(BF16), qh, nt_dims,
                                                preferred_element_type=F32)
        extract(work, PEER_NKEYS, vo, io)

    off = 0
    for a in range(k):
        w = k // (a + 1)
        cand[off:off + w, :] = v1[a:a + 1, :] + v2[0:w, :]
        candi[off:off + w, :] = i1[a:a + 1, :] * float(PEER_NKEYS) + i2[0:w, :]
        off += w
    n_rows = cand.shape[0]
    cand[off:n_rows, :] = jnp.full((n_rows - off, tmn), -jnp.inf, F32)
    candi[off:n_rows, :] = jnp.zeros((n_rows - off, tmn), F32)
    extract(cand, n_rows, ts_buf, te_buf, payload=candi)

    ts = ts_buf[...]
    ex = jnp.exp(ts - jnp.max(ts, axis=0, keepdims=True))
    g_ref[...] = ex / jnp.sum(ex, axis=0, keepdims=True)
    e_ref[...] = te_buf[...].astype(I32)


def _peer_topk(qp, sub_keys, tmn=PEER_TM):
    n = qp.shape[0]
    k = PEER_TOPK
    qd = 2 * PEER_HALF
    sk = sub_keys.reshape(PEER_HEADS, 2, PEER_NKEYS, PEER_HALF)
    assert n % tmn == 0
    rows = PEER_HEADS * k
    n_cand = sum(k // (a + 1) for a in range(k))
    cand_rows = -(-n_cand // SUBLANES) * SUBLANES
    small_f = pltpu.VMEM((k, tmn), F32)
    return pl.pallas_call(
        _peer_topk_body,
        grid=(n // tmn, PEER_HEADS),
        in_specs=[
            pl.BlockSpec((tmn, qd), lambda i, h: (i, h)),
            pl.BlockSpec((None, 2, PEER_NKEYS, PEER_HALF), lambda i, h: (h, 0, 0, 0)),
        ],
        out_specs=[
            pl.BlockSpec((k, tmn), lambda i, h: (h, i)),
            pl.BlockSpec((k, tmn), lambda i, h: (h, i)),
        ],
        out_shape=[
            jax.ShapeDtypeStruct((rows, n), I32),
            jax.ShapeDtypeStruct((rows, n), F32),
        ],
        scratch_shapes=[
            pltpu.VMEM((PEER_NKEYS, tmn), F32),
            small_f, small_f, small_f, small_f,
            pltpu.VMEM((cand_rows, tmn), F32), pltpu.VMEM((cand_rows, tmn), F32),
            small_f, small_f,
        ],
        compiler_params=_cparams(("parallel", "arbitrary")),
        name="peer_topk",
    )(qp, sk)


def _uv_pack_body(u_ref, v_ref, o_ref):
    te, d = u_ref.shape
    half = d // LANES
    o_ref[:, 0:half, :] = u_ref[...].reshape(te, half, LANES).astype(BF16)
    o_ref[:, half:2 * half, :] = v_ref[...].reshape(te, half, LANES).astype(BF16)


def _uv_pack(u, v, te=UV_TE):
    ne, d = u.shape
    half = d // LANES
    assert ne % te == 0 and d % LANES == 0
    return pl.pallas_call(
        _uv_pack_body,
        grid=(ne // te,),
        in_specs=[pl.BlockSpec((te, d), lambda i: (i, 0))] * 2,
        out_specs=pl.BlockSpec((te, 2 * half, LANES), lambda i: (i, 0, 0)),
        out_shape=jax.ShapeDtypeStruct((ne, 2 * half, LANES), BF16),
        compiler_params=_cparams(("parallel",)),
        name="uv_pack",
    )(u, v)


def _peer_mix_body(idx_cur, idx_nxt, uv_hbm, h_ref, gate_ref, x_ref, g2_ref, o_ref,
                   buf_a, buf_b, cbuf, sem, *, tb, ne):
    i = pl.program_id(0)
    nb = pl.num_programs(0)
    half = h_ref.shape[1] // LANES
    bufs = (buf_a, buf_b)

    def row_copy(e_idx, which, row):
        return pltpu.make_async_copy(uv_hbm.at[e_idx], bufs[which].at[row], sem.at[which])

    def block_wait(which):
        pltpu.make_async_copy(uv_hbm.at[pl.ds(0, tb * ne)], bufs[which], sem.at[which]).wait()

    def issue(idx_ref, row0, which, t):
        for e in range(ne):
            row_copy(idx_ref[row0 + t, e], which, t * ne + e).start(priority=e % 2)

    @pl.when(i == 0)
    def _():
        def issue_first(t, carry):
            issue(idx_cur, 0, 0, t)
            return carry
        lax.fori_loop(0, tb, issue_first, 0)

    lane = lax.broadcasted_iota(I32, (SUBLANES, LANES), 1)

    def phase(which, tok0, idx_ref, idx_row0):
        src = bufs[which]

        def u_side(t):
            h = h_ref[tok0 + t:tok0 + t + 1, :].reshape(half, LANES)
            acc = jnp.zeros((SUBLANES, LANES), F32)
            for e in range(ne):
                prod = src[t * ne + e, 0:half, :].astype(F32) * h
                part = prod[0:SUBLANES]
                for s in range(1, half // SUBLANES):
                    part = part + prod[s * SUBLANES:(s + 1) * SUBLANES]
                acc = jnp.where(lane == e, jnp.sum(part, axis=-1, keepdims=True), acc)
            a = jnp.sum(acc, axis=0, keepdims=True)
            coef = _gelu_tanh(a) * gate_ref[pl.ds(tok0 + t, 1), :]
            cbuf[t % 2] = jnp.transpose(jnp.broadcast_to(coef, (ne, ne)))

        def v_side(t):
            out = jnp.zeros((half, LANES), F32)
            for e in range(ne):
                out = out + cbuf[t % 2, e:e + 1, :] * src[t * ne + e, half:2 * half, :].astype(F32)
            row = slice(tok0 + t, tok0 + t + 1)
            o_ref[row, :] = x_ref[row, :] + g2_ref[...] * out.reshape(1, half * LANES)

        block_wait(which)
        u_side(0)
        for t in range(tb - 1):
            issue(idx_ref, idx_row0, 1 - which, t)
            v_side(t)
            u_side(t + 1)
        issue(idx_ref, idx_row0, 1 - which, tb - 1)
        v_side(tb - 1)

    phase(0, 0, idx_cur, tb)
    phase(1, tb, idx_nxt, 0)

    @pl.when(i == nb - 1)
    def _():
        block_wait(0)


def _peer_mix(idx, gate, uv, h2, x1, mod6, gate_idx, seq, tb=PEER_TB):
    n, ne = idx.shape
    d = h2.shape[1]
    half = d // LANES
    tb2 = 2 * tb
    nb = n // tb2
    assert n % tb2 == 0 and seq % tb2 == 0 and ne == LANES
    rows = pltpu.VMEM((tb * ne, 2 * half, LANES), BF16)
    return pl.pallas_call(
        functools.partial(_peer_mix_body, tb=tb, ne=ne),
        grid=(nb,),
        in_specs=[
            pl.BlockSpec((tb2, ne), lambda i: (i, 0), memory_space=pltpu.SMEM),
            pl.BlockSpec((tb2, ne), lambda i: (jnp.minimum(i + 1, nb - 1), 0), memory_space=pltpu.SMEM),
            pl.BlockSpec(memory_space=pl.ANY),
            pl.BlockSpec((tb2, d), lambda i: (i, 0)),
            pl.BlockSpec((tb2, ne), lambda i: (i, 0)),
            pl.BlockSpec((tb2, d), lambda i: (i, 0)),
            pl.BlockSpec((None, 1, d), lambda i: ((i * tb2 // seq) * 6 + gate_idx, 0, 0)),
        ],
        out_specs=pl.BlockSpec((tb2, d), lambda i: (i, 0)),
        out_shape=jax.ShapeDtypeStruct((n, d), F32),
        scratch_shapes=[rows, rows, pltpu.VMEM((2, ne, ne), F32), pltpu.SemaphoreType.DMA((2,))],
        compiler_params=_cparams(("arbitrary",)),
        name="peer_mix",
    )(idx, idx, uv, h2, gate, x1, mod6)


def _layer(x, c, norm1_g, norm2_g, w_ada, b_ada, w_in, b_in, q_norm_g, k_norm_g,
           cmp_pos_k, cmp_pos_v, cmp_k_w1, cmp_k_w2, cmp_v_w1, cmp_v_w2, w_nsa_out,
           conv_dw_w, conv_dw_b, conv_ln_g, conv_ln_b, conv_pw_w, conv_pw_b, w_out,
           peer_w_q, peer_sub_keys, peer_u, peer_v):
    b, s, d = x.shape
    n = b * s
    g = N_KV_GROUPS
    conv_ch = conv_dw_w.shape[1]
    x2 = x.reshape(n, d)

    rows = max(SUBLANES, b)
    sc = jnp.zeros((rows, d), F32).at[:b].set(c * _sigmoid(c))
    mod = _matmul_bias(sc, w_ada, b_ada.reshape(1, -1), tm=rows)[:b]
    mod6 = mod.reshape(b * 6, 1, d)

    sizes = (Q_COLS,) + (KV_COLS,) * 6 + (3 * N_HEADS, 2 * conv_ch, 2 * d)
    offs = [0]
    for sz in sizes:
        offs.append(offs[-1] + sz)
    part = lambda arr, k: arr[..., offs[k]:offs[k + 1]]
    per_group = 3 * HEADS_PER_GROUP

    def gate_cols(arr):
        a = part(arr, 7).reshape(arr.shape[:-1] + (g, per_group))
        a = jnp.pad(a, [(0, 0)] * (a.ndim - 1) + [(0, LANES - per_group)])
        return a.reshape(arr.shape[:-1] + (GATE_PAD,))

    order = [0, 8, 9, 1, 2, 3, 4, 5, 6]
    w_cat = jnp.concatenate([part(w_in, k) for k in order] + [gate_cols(w_in)], axis=-1).astype(BF16)
    b_cat = jnp.concatenate([part(b_in, k) for k in order] + [gate_cols(b_in)], axis=-1).reshape(1, -1)
    col_q = 0
    col_glu = Q_COLS
    col_merge = col_glu + 2 * conv_ch
    col_cmp = col_merge + 2 * d
    col_slc = col_cmp + 2 * KV_COLS
    col_win = col_slc + 2 * KV_COLS
    col_gate = col_win + 2 * KV_COLS

    z, = _matmul_norm(x2, norm1_g.reshape(1, d), mod6, 0, 1, s, w_cat, b_cat, want_f32=False, tm=MM_TM)

    qn, kvn = _head_norm(z, col_q, col_slc, col_win, q_norm_g.reshape(1, -1),
                         k_norm_g[1].reshape(1, -1), k_norm_g[2].reshape(1, -1))
    kcn = _compress(z, col_cmp, cmp_pos_k, cmp_k_w1, cmp_k_w2, k_norm_g[0].reshape(1, -1), True, b, s)
    vcm = _compress(z, col_cmp + KV_COLS, cmp_pos_v, cmp_v_w1, cmp_v_w2, k_norm_g[0].reshape(1, -1), False, b, s)

    attn = _nsa(qn, kvn, kcn, vcm, z, col_gate, b, s)
    y_attn = _matmul_bias(attn, w_nsa_out, jnp.zeros((1, d), F32))

    yc = _conv_module(z, col_glu, col_glu + conv_ch, conv_dw_w, conv_dw_b.reshape(1, -1),
                      conv_ln_g.reshape(1, -1), conv_ln_b.reshape(1, -1), b, s)
    merged = _matmul_merge(yc, conv_pw_w, conv_pw_b.reshape(1, -1), z, col_merge, col_merge + d, y_attn)
    x1 = _matmul_resid(merged, w_out, x2, mod6, 2, s)

    qp, h2f = _matmul_norm(x1, norm2_g.reshape(1, d), mod6, 3, 4, s, peer_w_q,
                           jnp.zeros((1, peer_w_q.shape[1]), F32), want_f32=True, tm=MM_TM // 2)
    e_t, g_t = _peer_topk(qp, peer_sub_keys)
    idx = e_t.T
    gate = g_t.T
    uv = _uv_pack(peer_u, peer_v)
    out = _peer_mix(idx, gate, uv, h2f, x1, mod6, 5, s)
    return out.reshape(b, s, d)


def kernel(x, c, norm1_g, norm2_g, w_ada, b_ada, w_in, b_in, q_norm_g, k_norm_g, cmp_pos_k, cmp_pos_v,
           cmp_k_w1, cmp_k_w2, cmp_v_w1, cmp_v_w2, w_nsa_out, conv_dw_w, conv_dw_b, conv_ln_g, conv_ln_b,
           conv_pw_w, conv_pw_b, w_out, peer_w_q, peer_sub_keys, peer_u, peer_v):
    depth = norm1_g.shape[0]
    for l in range(depth):
        x = _layer(x, c, norm1_g[l], norm2_g[l], w_ada[l], b_ada[l], w_in[l], b_in[l],
                   q_norm_g[l], k_norm_g[l], cmp_pos_k[l], cmp_pos_v[l],
                   cmp_k_w1[l], cmp_k_w2[l], cmp_v_w1[l], cmp_v_w2[l], w_nsa_out[l],
                   conv_dw_w[l], conv_dw_b[l], conv_ln_g[l], conv_ln_b[l],
                   conv_pw_w[l], conv_pw_b[l], w_out[l],
                   peer_w_q[l], peer_sub_keys[l], peer_u[l], peer_v[l])
    return x
```

```python
import functools
import math

import jax
import jax.numpy as jnp
from jax import lax
from jax.experimental import pallas as pl
from jax.experimental.pallas import tpu as pltpu

F32 = jnp.float32
BF16 = jnp.bfloat16
I32 = jnp.int32

N_HEADS = 16
N_KV_GROUPS = 4
HEADS_PER_GROUP = N_HEADS // N_KV_GROUPS
HEAD_DIM = 128
L_CMP = 32
STRIDE_CMP = 16
L_SEL = 64
N_SEL = 16
WINDOW = 512
CONV_WIDTH = 31
PEER_HEADS = 8
PEER_NKEYS = 128
PEER_HALF = 128
PEER_TOPK = 16
EPS = 1e-6
NEG_BIG = 1e30

LANES = 128
SUBLANES = 8
VMEM_LIMIT = 48 * 1024 * 1024

MM_TM = 1024
MM_TN = 512
ROW_TILE = 256
NSA_TQ = 256
NSA_TK = 1024
CONV_TS = 256
CONV_HALO = 32
PEER_TM = 1024
PEER_TB = 8
PEER_NBUF = 4
PEER_AHEAD = 2
UV_TE = 512

Q_COLS = N_HEADS * HEAD_DIM
KV_COLS = N_KV_GROUPS * HEAD_DIM
GATE_PAD = N_KV_GROUPS * LANES


def _cparams(sem):
    return pltpu.CompilerParams(dimension_semantics=sem, vmem_limit_bytes=VMEM_LIMIT)


def _gelu_tanh(x):
    c = math.sqrt(2.0 / math.pi)
    return x * (0.5 * (1.0 + jnp.tanh(c * (x + 0.044715 * (x * x * x)))))


def _sigmoid(x):
    return 1.0 / (1.0 + jnp.exp(-x))


def _mm_bias_body(a_ref, b_ref, bias_ref, o_ref):
    acc = jnp.dot(a_ref[...].astype(BF16), b_ref[...].astype(BF16), preferred_element_type=F32)
    o_ref[...] = (acc + bias_ref[...]).astype(o_ref.dtype)


def _matmul_bias(a, b, bias, out_dtype=F32, tm=MM_TM, tn=MM_TN):
    m, k = a.shape
    n = b.shape[1]
    tm = min(tm, m)
    tn = min(tn, n)
    assert m % tm == 0 and n % tn == 0
    return pl.pallas_call(
        _mm_bias_body,
        grid=(m // tm, n // tn),
        in_specs=[
            pl.BlockSpec((tm, k), lambda i, j: (i, 0)),
            pl.BlockSpec((k, tn), lambda i, j: (0, j)),
            pl.BlockSpec((1, tn), lambda i, j: (0, j)),
        ],
        out_specs=pl.BlockSpec((tm, tn), lambda i, j: (i, j)),
        out_shape=jax.ShapeDtypeStruct((m, n), out_dtype),
        compiler_params=_cparams(("parallel", "arbitrary")),
        name="mm_bias",
    )(a, b, bias)


def _mm_merge_body(a_ref, b_ref, bias_ref, g0_ref, g1_ref, ya_ref, o_ref):
    acc = jnp.dot(a_ref[...], b_ref[...].astype(BF16), preferred_element_type=F32) + bias_ref[...]
    o_ref[...] = (_sigmoid(g0_ref[...]) * ya_ref[...] + _sigmoid(g1_ref[...]) * acc).astype(o_ref.dtype)


def _matmul_merge(a, b, bias, z, g0_col, g1_col, y_attn, tm=MM_TM, tn=MM_TN):
    m, k = a.shape
    n = b.shape[1]
    assert m % tm == 0 and n % tn == 0 and g0_col % tn == 0 and g1_col % tn == 0
    return pl.pallas_call(
        _mm_merge_body,
        grid=(m // tm, n // tn),
        in_specs=[
            pl.BlockSpec((tm, k), lambda i, j: (i, 0)),
            pl.BlockSpec((k, tn), lambda i, j: (0, j)),
            pl.BlockSpec((1, tn), lambda i, j: (0, j)),
            pl.BlockSpec((tm, tn), lambda i, j: (i, g0_col // tn + j)),
            pl.BlockSpec((tm, tn), lambda i, j: (i, g1_col // tn + j)),
            pl.BlockSpec((tm, tn), lambda i, j: (i, j)),
        ],
        out_specs=pl.BlockSpec((tm, tn), lambda i, j: (i, j)),
        out_shape=jax.ShapeDtypeStruct((m, n), BF16),
        compiler_params=_cparams(("parallel", "arbitrary")),
        name="mm_merge",
    )(a, b, bias, z, z, y_attn)


def _mm_resid_body(a_ref, b_ref, res_ref, gate_ref, o_ref):
    acc = jnp.dot(a_ref[...], b_ref[...].astype(BF16), preferred_element_type=F32)
    o_ref[...] = res_ref[...] + gate_ref[...] * acc


def _matmul_resid(a, b, res, mod6, gate_idx, seq, tm=MM_TM, tn=MM_TN):
    m, k = a.shape
    n = b.shape[1]
    assert m % tm == 0 and n % tn == 0 and seq % tm == 0
    return pl.pallas_call(
        _mm_resid_body,
        grid=(m // tm, n // tn),
        in_specs=[
            pl.BlockSpec((tm, k), lambda i, j: (i, 0)),
            pl.BlockSpec((k, tn), lambda i, j: (0, j)),
            pl.BlockSpec((tm, tn), lambda i, j: (i, j)),
            pl.BlockSpec((None, 1, tn), lambda i, j: ((i * tm // seq) * 6 + gate_idx, 0, j)),
        ],
        out_specs=pl.BlockSpec((tm, tn), lambda i, j: (i, j)),
        out_shape=jax.ShapeDtypeStruct((m, n), F32),
        compiler_params=_cparams(("parallel", "arbitrary")),
        name="mm_resid",
    )(a, b, res, mod6)


def _mm_norm_body(x_ref, g_ref, shift_ref, scale_ref, b_ref, bias_ref, o_ref, *rest, want_f32):
    hb_ref = rest[-1]

    @pl.when(pl.program_id(1) == 0)
    def _():
        x = x_ref[...]
        y = x * lax.rsqrt(jnp.mean(x * x, axis=-1, keepdims=True) + EPS)
        h = (y * g_ref[...]) * (1.0 + scale_ref[...]) + shift_ref[...]
        hb_ref[...] = h.astype(BF16)
        if want_f32:
            rest[0][...] = h

    acc = jnp.dot(hb_ref[...], b_ref[...].astype(BF16), preferred_element_type=F32)
    o_ref[...] = acc + bias_ref[...]


def _matmul_norm(x2d, g, mod6, shift_idx, scale_idx, seq, b, bias, want_f32, tm, tn=MM_TN):
    m, k = x2d.shape
    n = b.shape[1]
    assert m % tm == 0 and n % tn == 0 and seq % tm == 0
    out_shape = [jax.ShapeDtypeStruct((m, n), F32)]
    out_specs = [pl.BlockSpec((tm, tn), lambda i, j: (i, j))]
    if want_f32:
        out_shape.append(jax.ShapeDtypeStruct((m, k), F32))
        out_specs.append(pl.BlockSpec((tm, k), lambda i, j: (i, 0)))
    return pl.pallas_call(
        functools.partial(_mm_norm_body, want_f32=want_f32),
        grid=(m // tm, n // tn),
        in_specs=[
            pl.BlockSpec((tm, k), lambda i, j: (i, 0)),
            pl.BlockSpec((1, k), lambda i, j: (0, 0)),
            pl.BlockSpec((None, 1, k), lambda i, j: ((i * tm // seq) * 6 + shift_idx, 0, 0)),
            pl.BlockSpec((None, 1, k), lambda i, j: ((i * tm // seq) * 6 + scale_idx, 0, 0)),
            pl.BlockSpec((k, tn), lambda i, j: (0, j)),
            pl.BlockSpec((1, tn), lambda i, j: (0, j)),
        ],
        out_specs=out_specs,
        out_shape=out_shape,
        scratch_shapes=[pltpu.VMEM((tm, k), BF16)],
        compiler_params=_cparams(("parallel", "arbitrary")),
        name="mm_norm",
    )(x2d, g, mod6, mod6, b, bias)


def _head_norm_body(q_ref, sl_ref, wn_ref, gq_ref, gs_ref, gw_ref, qn_ref, kv_ref):
    def rms(x, g):
        y = x * lax.rsqrt(jnp.mean(x * x, axis=-1, keepdims=True) + EPS)
        return y * g

    qscale = HEAD_DIM ** -0.5
    for c in range(N_HEADS):
        sl = slice(c * HEAD_DIM, (c + 1) * HEAD_DIM)
        qn_ref[:, sl] = (rms(q_ref[:, sl], gq_ref[...]) * qscale).astype(BF16)
    for g in range(N_KV_GROUPS):
        sl = slice(g * HEAD_DIM, (g + 1) * HEAD_DIM)
        sv = slice(KV_COLS + g * HEAD_DIM, KV_COLS + (g + 1) * HEAD_DIM)
        kv_ref[:, sl] = rms(sl_ref[:, sl], gs_ref[...]).astype(BF16)
        kv_ref[:, sv] = sl_ref[:, sv].astype(BF16)
        kv_ref[:, slice(2 * KV_COLS + g * HEAD_DIM, 2 * KV_COLS + (g + 1) * HEAD_DIM)] = (
            rms(wn_ref[:, sl], gw_ref[...]).astype(BF16))
        kv_ref[:, slice(3 * KV_COLS + g * HEAD_DIM, 3 * KV_COLS + (g + 1) * HEAD_DIM)] = (
            wn_ref[:, sv].astype(BF16))


def _head_norm(z, col_q, col_slc, col_win, gq, gs, gw, tr=ROW_TILE):
    n = z.shape[0]
    w2 = 2 * KV_COLS
    assert n % tr == 0 and col_q % Q_COLS == 0 and col_slc % w2 == 0 and col_win % w2 == 0
    return pl.pallas_call(
        _head_norm_body,
        grid=(n // tr,),
        in_specs=[
            pl.BlockSpec((tr, Q_COLS), lambda i: (i, col_q // Q_COLS)),
            pl.BlockSpec((tr, w2), lambda i: (i, col_slc // w2)),
            pl.BlockSpec((tr, w2), lambda i: (i, col_win // w2)),
            pl.BlockSpec((1, HEAD_DIM), lambda i: (0, 0)),
            pl.BlockSpec((1, HEAD_DIM), lambda i: (0, 0)),
            pl.BlockSpec((1, HEAD_DIM), lambda i: (0, 0)),
        ],
        out_specs=[
            pl.BlockSpec((tr, Q_COLS), lambda i: (i, 0)),
            pl.BlockSpec((tr, 4 * KV_COLS), lambda i: (i, 0)),
        ],
        out_shape=[
            jax.ShapeDtypeStruct((n, Q_COLS), BF16),
            jax.ShapeDtypeStruct((n, 4 * KV_COLS), BF16),
        ],
        compiler_params=_cparams(("parallel",)),
        name="head_norm",
    )(z, z, z, gq, gs, gw)


def _compress_body(z_ref, pa_ref, pb_ref, wt_ref, wb_ref, w2_ref, g_ref, o_ref, *, normalize):
    n_chunk = z_ref.shape[0] // STRIDE_CMP
    top = jnp.zeros((n_chunk, HEAD_DIM), F32)
    bot = jnp.zeros((n_chunk, HEAD_DIM), F32)
    for l in range(STRIDE_CMP):
        rows = z_ref[pl.ds(l, n_chunk, stride=STRIDE_CMP), :]
        wl = slice(l * HEAD_DIM, (l + 1) * HEAD_DIM)
        top = top + jnp.dot((rows + pa_ref[l:l + 1, :]).astype(BF16), wt_ref[wl, :].astype(BF16),
                            preferred_element_type=F32)
        bot = bot + jnp.dot((rows + pb_ref[l:l + 1, :]).astype(BF16), wb_ref[wl, :].astype(BF16),
                            preferred_element_type=F32)
    pre = top + pltpu.roll(bot, n_chunk - 1, 0)
    out = jnp.dot(_gelu_tanh(pre).astype(BF16), w2_ref[...].astype(BF16), preferred_element_type=F32)
    if normalize:
        out = out * lax.rsqrt(jnp.mean(out * out, axis=-1, keepdims=True) + EPS) * g_ref[...]
    o_ref[...] = out.astype(BF16)


def _compress(z, col, pos, w1, w2, gain, normalize, batch, seq):
    g = N_KV_GROUPS
    n_chunk = seq // STRIDE_CMP
    half = STRIDE_CMP * HEAD_DIM
    assert col % HEAD_DIM == 0 and seq % STRIDE_CMP == 0
    full = lambda *shape: pl.BlockSpec(shape, lambda bi, gi: (0,) * len(shape))
    return pl.pallas_call(
        functools.partial(_compress_body, normalize=normalize),
        grid=(batch, g),
        in_specs=[
            pl.BlockSpec((seq, HEAD_DIM), lambda bi, gi: (bi, col // HEAD_DIM + gi)),
            full(STRIDE_CMP, HEAD_DIM), full(STRIDE_CMP, HEAD_DIM),
            full(half, HEAD_DIM), full(half, HEAD_DIM), full(HEAD_DIM, HEAD_DIM), full(1, HEAD_DIM),
        ],
        out_specs=pl.BlockSpec((None, None, n_chunk, HEAD_DIM), lambda bi, gi: (bi, gi, 0, 0)),
        out_shape=jax.ShapeDtypeStruct((batch, g, n_chunk, HEAD_DIM), BF16),
        compiler_params=_cparams(("parallel", "parallel")),
        name="compress",
    )(z, pos[:STRIDE_CMP], pos[STRIDE_CMP:], w1[:half], w1[half:], w2, gain)


def _nsa_body(q_ref, kc_ref, vc_ref, ks_ref, vs_ref, kw_ref, vw_ref, g_ref, o_ref,
              qs_ref, sc_ref, oc_ref, *, tq, tk, seq):
    n_cmp = (seq - L_CMP) // STRIDE_CMP + 1
    n_blk = seq // L_SEL
    n_sel = min(N_SEL, n_blk)
    ncp = kc_ref.shape[0]
    hpg = HEADS_PER_GROUP
    wk = WINDOW + tq
    nt_dims = (((1,), (1,)), ((), ()))
    masked = -NEG_BIG
    m_floor = -0.1 * NEG_BIG

    t0 = pl.program_id(2) * tq
    t_col = t0 + lax.broadcasted_iota(I32, (tq, 1), 0)

    for h in range(hpg):
        qs_ref[h * tq:(h + 1) * tq, :] = q_ref[:, h * HEAD_DIM:(h + 1) * HEAD_DIM]
    q_all = qs_ref[...]

    def per_head(x):
        return jnp.concatenate([x] * hpg, axis=0)

    def softmax_terms(s):
        m = jnp.maximum(jnp.max(s, axis=-1, keepdims=True), m_floor)
        e = jnp.exp(s - m)
        return e, jnp.sum(e, axis=-1, keepdims=True)

    gates = _sigmoid(g_ref[...])
    gate_col = lambda c: jnp.concatenate([gates[:, 3 * h + c:3 * h + c + 1] for h in range(hpg)], axis=0)

    n_row = lax.broadcasted_iota(I32, (1, ncp), 1)
    vis = (n_row * STRIDE_CMP + (L_CMP - 1) <= t_col) & (n_row < n_cmp)
    s = lax.dot_general(q_all, kc_ref[...], nt_dims, preferred_element_type=F32)
    e, l = softmax_terms(s + per_head(jnp.where(vis, 0.0, masked)))
    p = e / jnp.maximum(l, 1e-30)
    oc_ref[...] = gate_col(0) * jnp.dot(p.astype(BF16), vc_ref[...], preferred_element_type=F32)
    psum = p[0:tq]
    for h in range(1, hpg):
        psum = psum + p[h * tq:(h + 1) * tq]

    w0 = pl.multiple_of(jnp.minimum(jnp.maximum(t0 - WINDOW, 0), seq - wk), tq)
    dist = t_col - (w0 + lax.broadcasted_iota(I32, (1, wk), 1))
    bias_w = jnp.where(dist >= 0, jnp.where(dist < WINDOW, 0.0, masked), masked)
    s = lax.dot_general(q_all, kw_ref[pl.ds(w0, wk), :], nt_dims, preferred_element_type=F32)
    e, l = softmax_terms(s + per_head(bias_w))
    o_win = jnp.dot(e.astype(BF16), vw_ref[pl.ds(w0, wk), :], preferred_element_type=F32)
    oc_ref[...] = oc_ref[...] + gate_col(2) * (o_win / l)

    j_row = lax.broadcasted_iota(I32, (1, LANES), 1)
    n_colv = lax.broadcasted_iota(I32, (ncp, 1), 0)
    overlap = ((n_colv * STRIDE_CMP < j_row * L_SEL + L_SEL)
               & (n_colv * STRIDE_CMP + L_CMP > j_row * L_SEL)
               & (n_colv < n_cmp) & (j_row < n_blk))
    overlap = jnp.where(overlap, 1.0, 0.0).astype(BF16)
    p_hi = psum.astype(BF16)
    r1 = psum - p_hi.astype(F32)
    p_mid = r1.astype(BF16)
    p_lo = (r1 - p_mid.astype(F32)).astype(BF16)
    imp = (jnp.dot(p_hi, overlap, preferred_element_type=F32)
           + jnp.dot(p_mid, overlap, preferred_element_type=F32)
           + jnp.dot(p_lo, overlap, preferred_element_type=F32))

    sel_shift = L_SEL.bit_length() - 1
    cur = jnp.right_shift(t_col, sel_shift)
    forced = (j_row == 0) | (j_row == cur) | (j_row == cur - 1)
    score = jnp.where(forced, NEG_BIG, jnp.where(j_row * L_SEL <= t_col, imp, -NEG_BIG))
    sc_ref[...] = jnp.transpose(score)
    n_grp = n_blk // SUBLANES
    groups = [sc_ref[g * SUBLANES:(g + 1) * SUBLANES, :] for g in range(n_grp)]
    j_sub = lax.broadcasted_iota(I32, (SUBLANES, tq), 0)
    ranks = [jnp.zeros((SUBLANES, tq), F32) for _ in range(n_grp)]
    for i in range(n_blk):
        rb = jnp.broadcast_to(sc_ref[i:i + 1, :], (SUBLANES, tq))
        for g in range(n_grp):
            lo = g * SUBLANES
            if lo > i:
                beats = jnp.where(rb >= groups[g], 1.0, 0.0)
            elif lo + SUBLANES - 1 <= i:
                beats = jnp.where(rb > groups[g], 1.0, 0.0)
            else:
                beats = jnp.where(j_sub + lo > i, jnp.where(rb >= groups[g], 1.0, 0.0),
                                  jnp.where(rb > groups[g], 1.0, 0.0))
            ranks[g] = ranks[g] + beats
    sel_t = jnp.concatenate([jnp.where(r < n_sel, 1.0, 0.0) for r in ranks]
                            + [jnp.zeros((LANES - n_blk, tq), F32)], axis=0)
    sel = jnp.transpose(sel_t).astype(BF16)

    c_row = lax.broadcasted_iota(I32, (1, tk), 1)
    j_colv = lax.broadcasted_iota(I32, (LANES, 1), 0)
    ones_cols = jnp.ones((tk, HEAD_DIM), BF16)
    rows = hpg * tq

    def slc_block(jb, state):
        m_old, l_old, acc_old = state
        k0 = jb * tk
        kpos = k0 + c_row
        expand = jnp.where(j_colv == jnp.right_shift(kpos, sel_shift), 1.0, 0.0).astype(BF16)
        chosen = jnp.dot(sel, expand, preferred_element_type=F32)
        bias = jnp.where(kpos <= t_col, jnp.where(chosen > 0.5, 0.0, masked), masked)
        s = lax.dot_general(q_all, ks_ref[k0:k0 + tk, :], nt_dims, preferred_element_type=F32)
        s = s + per_head(bias)
        m_new = jnp.maximum(m_old, jnp.max(s, axis=-1, keepdims=True))
        p = jnp.exp(s - m_new).astype(BF16)
        alpha = jnp.broadcast_to(jnp.exp(m_old - m_new), (rows, HEAD_DIM))
        pv = jnp.dot(p, jnp.concatenate([vs_ref[k0:k0 + tk, :], ones_cols], axis=1), preferred_element_type=F32)
        return m_new, alpha * l_old + pv[:, HEAD_DIM:], alpha * acc_old + pv[:, :HEAD_DIM]

    n_blocks = lax.div(t0 + (tq + tk - 1), tk)
    for nb in range(1, seq // tk + 1):
        @pl.when(n_blocks == nb)
        def _(nb=nb):
            state = (jnp.full((rows, 1), m_floor, F32), jnp.zeros((rows, HEAD_DIM), F32),
                     jnp.zeros((rows, HEAD_DIM), F32))
            for jb in range(nb):
                state = slc_block(jb, state)
            out = oc_ref[...] + gate_col(1) * (state[2] / state[1])
            for h in range(hpg):
                o_ref[:, h * HEAD_DIM:(h + 1) * HEAD_DIM] = out[h * tq:(h + 1) * tq].astype(BF16)


def _nsa(qn, kvn, kcn, vcm, z, col_gate, batch, seq, tq=NSA_TQ, tk=NSA_TK):
    n = qn.shape[0]
    g = N_KV_GROUPS
    gw = HEADS_PER_GROUP * HEAD_DIM
    nq = seq // tq
    ncp = kcn.shape[2]
    rows = HEADS_PER_GROUP * tq
    assert seq % tq == 0 and seq % tk == 0 and col_gate % LANES == 0
    assert seq // L_SEL <= LANES and (seq // L_SEL) % SUBLANES == 0 and seq >= WINDOW + tq
    kv_spec = lambda part: pl.BlockSpec((seq, HEAD_DIM), lambda b, gi, i: (b, part * g + gi))
    cmp_spec = pl.BlockSpec((None, None, ncp, HEAD_DIM), lambda b, gi, i: (b, gi, 0, 0))
    return pl.pallas_call(
        functools.partial(_nsa_body, tq=tq, tk=tk, seq=seq),
        grid=(batch, g, nq),
        in_specs=[
            pl.BlockSpec((tq, gw), lambda b, gi, i: (b * nq + i, gi)),
            cmp_spec, cmp_spec,
            kv_spec(0), kv_spec(1), kv_spec(2), kv_spec(3),
            pl.BlockSpec((tq, LANES), lambda b, gi, i: (b * nq + i, col_gate // LANES + gi)),
        ],
        out_specs=pl.BlockSpec((tq, gw), lambda b, gi, i: (b * nq + i, gi)),
        out_shape=jax.ShapeDtypeStruct((n, Q_COLS), BF16),
        scratch_shapes=[pltpu.VMEM((rows, HEAD_DIM), BF16), pltpu.VMEM((LANES, tq), F32),
                        pltpu.VMEM((rows, HEAD_DIM), F32)],
        compiler_params=_cparams(("parallel", "parallel", "arbitrary")),
        name="nsa",
    )(qn, kcn, vcm, kvn, kvn, kvn, kvn, z)


def _conv_body(u_ref, v_ref, hu_ref, hv_ref, w_ref, b_ref, lg_ref, lb_ref, o_ref, ubuf, ybuf, *, ts, halo):
    i = pl.program_id(1)
    hal = hu_ref[...] * _sigmoid(hv_ref[...])
    ubuf[0:halo, :] = jnp.where(i > 0, hal, 0.0)
    ubuf[halo:halo + ts, :] = u_ref[...] * _sigmoid(v_ref[...])
    ch = u_ref.shape[1]
    off = halo - (CONV_WIDTH - 1)
    for c in range(ch // LANES):
        sl = slice(c * LANES, (c + 1) * LANES)
        acc = jnp.zeros((ts, LANES), F32)
        for k in range(CONV_WIDTH):
            acc = acc + ubuf[off + k:off + k + ts, sl] * w_ref[k:k + 1, sl]
        ybuf[:, sl] = acc + b_ref[:, sl]
    y = ybuf[...]
    mu = jnp.mean(y, axis=-1, keepdims=True)
    yc = y - mu
    yn = yc * lax.rsqrt(jnp.mean(yc * yc, axis=-1, keepdims=True) + EPS) * lg_ref[...] + lb_ref[...]
    o_ref[...] = (yn * _sigmoid(yn)).astype(BF16)


def _conv_module(z, col_u, col_v, dw_w, dw_b, ln_g, ln_b, batch, seq, ts=CONV_TS, halo=CONV_HALO):
    n = z.shape[0]
    ch = dw_w.shape[1]
    ns = seq // ts
    r = ts // halo
    assert seq % ts == 0 and ts % halo == 0 and col_u % ch == 0 and col_v % ch == 0
    halo_map = lambda col: (lambda b, i: (jnp.maximum((b * ns + i) * r - 1, 0), col // ch))
    full = lambda *shape: pl.BlockSpec(shape, lambda b, i: (0,) * len(shape))
    return pl.pallas_call(
        functools.partial(_conv_body, ts=ts, halo=halo),
        grid=(batch, ns),
        in_specs=[
            pl.BlockSpec((ts, ch), lambda b, i: (b * ns + i, col_u // ch)),
            pl.BlockSpec((ts, ch), lambda b, i: (b * ns + i, col_v // ch)),
            pl.BlockSpec((halo, ch), halo_map(col_u)),
            pl.BlockSpec((halo, ch), halo_map(col_v)),
            full(CONV_WIDTH, ch), full(1, ch), full(1, ch), full(1, ch),
        ],
        out_specs=pl.BlockSpec((ts, ch), lambda b, i: (b * ns + i, 0)),
        out_shape=jax.ShapeDtypeStruct((n, ch), BF16),
        scratch_shapes=[pltpu.VMEM((ts + halo, ch), F32), pltpu.VMEM((ts, ch), F32)],
        compiler_params=_cparams(("parallel", "arbitrary")),
        name="conv_module",
    )(z, z, z, z, dw_w, dw_b, ln_g, ln_b)


def _peer_topk_body(q_ref, sk_ref, e_ref, g_ref, work, v1, i1, v2, i2, cand, candi, ts_buf, te_buf):
    k = PEER_TOPK
    tmn = q_ref.shape[0]
    nt_dims = (((1,), (1,)), ((), ()))

    def extract(src, nrows, val_out, idx_out, payload=None):
        rows = lax.broadcasted_iota(I32, (nrows, tmn), 0).astype(F32)

        def body(r, carry):
            x = src[0:nrows, :]
            m = jnp.max(x, axis=0, keepdims=True)
            pos = jnp.min(jnp.where(x == m, rows, float(nrows)), axis=0, keepdims=True)
            hit = rows == pos
            val_out[pl.ds(r, 1), :] = m
            if payload is None:
                idx_out[pl.ds(r, 1), :] = pos
            else:
                idx_out[pl.ds(r, 1), :] = jnp.max(jnp.where(hit, payload[...], -1.0), axis=0, keepdims=True)
            src[0:nrows, :] = jnp.where(hit, -jnp.inf, x)
            return carry

        lax.fori_loop(0, k, body, 0)

    for p, (vo, io) in enumerate(((v1, i1), (v2, i2))):
        qh = q_ref[:, p * PEER_HALF:(p + 1) * PEER_HALF].astype(BF16)
        work[0:PEER_NKEYS, :] = lax.dot_general(sk_ref[p].astype(BF16), qh, nt_dims,
                                                preferred_element_type=F32)
        extract(work, PEER_NKEYS, vo, io)

    off = 0
    for a in range(k):
        w = k // (a + 1)
        cand[off:off + w, :] = v1[a:a + 1, :] + v2[0:w, :]
        candi[off:off + w, :] = i1[a:a + 1, :] * float(PEER_NKEYS) + i2[0:w, :]
        off += w
    n_rows = cand.shape[0]
    cand[off:n_rows, :] = jnp.full((n_rows - off, tmn), -jnp.inf, F32)
    candi[off:n_rows, :] = jnp.zeros((n_rows - off, tmn), F32)
    extract(cand, n_rows, ts_buf, te_buf, payload=candi)

    ts = ts_buf[...]
    ex = jnp.exp(ts - jnp.max(ts, axis=0, keepdims=True))
    g_ref[...] = ex / jnp.sum(ex, axis=0, keepdims=True)
    e_ref[...] = te_buf[...].astype(I32)


def _peer_topk(qp, sub_keys, tmn=PEER_TM):
    n = qp.shape[0]
    k = PEER_TOPK
    qd = 2 * PEER_HALF
    sk = sub_keys.reshape(PEER_HEADS, 2, PEER_NKEYS, PEER_HALF)
    assert n % tmn == 0
    rows = PEER_HEADS * k
    n_cand = sum(k // (a + 1) for a in range(k))
    cand_rows = -(-n_cand // SUBLANES) * SUBLANES
    small_f = pltpu.VMEM((k, tmn), F32)
    return pl.pallas_call(
        _peer_topk_body,
        grid=(n // tmn, PEER_HEADS),
        in_specs=[
            pl.BlockSpec((tmn, qd), lambda i, h: (i, h)),
            pl.BlockSpec((None, 2, PEER_NKEYS, PEER_HALF), lambda i, h: (h, 0, 0, 0)),
        ],
        out_specs=[
            pl.BlockSpec((k, tmn), lambda i, h: (h, i)),
            pl.BlockSpec((k, tmn), lambda i, h: (h, i)),
        ],
        out_shape=[
            jax.ShapeDtypeStruct((rows, n), I32),
            jax.ShapeDtypeStruct((rows, n), F32),
        ],
        scratch_shapes=[
            pltpu.VMEM((PEER_NKEYS, tmn), F32),
            small_f, small_f, small_f, small_f,
            pltpu.VMEM((cand_rows, tmn), F32), pltpu.VMEM((cand_rows, tmn), F32),
            small_f, small_f,
        ],
        compiler_params=_cparams(("parallel", "arbitrary")),
        name="peer_topk",
    )(qp, sk)


def _uv_pack_body(u_ref, v_ref, o_ref):
    te, d = u_ref.shape
    half = d // LANES
    o_ref[:, 0:half, :] = u_ref[...].reshape(te, half, LANES).astype(BF16)
    o_ref[:, half:2 * half, :] = v_ref[...].reshape(te, half, LANES).astype(BF16)


def _uv_pack(u, v, te=UV_TE):
    ne, d = u.shape
    half = d // LANES
    assert ne % te == 0 and d % LANES == 0
    return pl.pallas_call(
        _uv_pack_body,
        grid=(ne // te,),
        in_specs=[pl.BlockSpec((te, d), lambda i: (i, 0))] * 2,
        out_specs=pl.BlockSpec((te, 2 * half, LANES), lambda i: (i, 0, 0)),
        out_shape=jax.ShapeDtypeStruct((ne, 2 * half, LANES), BF16),
        compiler_params=_cparams(("parallel",)),
        name="uv_pack",
    )(u, v)


def _peer_mix_body(idx_cur, idx_nxt, uv_hbm, h_ref, gate_ref, x_ref, g2_ref, o_ref, *scratch, tb, ne, nbuf):
    bufs, cbuf, sem = scratch[:nbuf], scratch[nbuf], scratch[nbuf + 1]
    i = pl.program_id(0)
    nb = pl.num_programs(0)
    half = h_ref.shape[1]

    def row_copy(e_idx, which, row):
        return pltpu.make_async_copy(uv_hbm.at[e_idx], bufs[which].at[row], sem.at[which])

    def block_wait(which):
        pltpu.make_async_copy(uv_hbm.at[pl.ds(0, tb * ne)], bufs[which], sem.at[which]).wait()

    def issue(idx_ref, row0, which, t):
        for e in range(ne):
            row_copy(idx_ref[row0 + t, e], which, t * ne + e).start(priority=e % 2)

    @pl.when(i == 0)
    def _():
        for p in range(PEER_AHEAD):
            def issue_first(t, carry, p=p):
                issue(idx_cur, p * tb, p, t)
                return carry
            lax.fori_loop(0, tb, issue_first, 0)

    lane = lax.broadcasted_iota(I32, (SUBLANES, LANES), 1)

    def phase(which, tok0, idx_ref, idx_row0, dst):
        src = bufs[which]

        def u_side(t):
            h = h_ref[tok0 + t]
            acc = jnp.zeros((SUBLANES, LANES), F32)
            for e in range(ne):
                prod = src[t * ne + e, 0:half, :].astype(F32) * h
                part = prod[0:SUBLANES]
                for s in range(1, half // SUBLANES):
                    part = part + prod[s * SUBLANES:(s + 1) * SUBLANES]
                acc = jnp.where(lane == e, jnp.sum(part, axis=-1, keepdims=True), acc)
            a = jnp.sum(acc, axis=0, keepdims=True)
            coef = _gelu_tanh(a) * gate_ref[pl.ds(tok0 + t, 1), :]
            cbuf[t % 2] = jnp.transpose(jnp.broadcast_to(coef, (ne, ne)))

        def v_side(t):
            out = jnp.zeros((half, LANES), F32)
            for e in range(ne):
                out = out + cbuf[t % 2, e:e + 1, :] * src[t * ne + e, half:2 * half, :].astype(F32)
            o_ref[tok0 + t] = x_ref[tok0 + t] + g2_ref[...] * out

        block_wait(which)
        u_side(0)
        for t in range(tb - 1):
            issue(idx_ref, idx_row0, dst, t)
            v_side(t)
            u_side(t + 1)
        issue(idx_ref, idx_row0, dst, tb - 1)
        v_side(tb - 1)

    for p in range(nbuf):
        ahead = p + PEER_AHEAD
        if ahead < nbuf:
            phase(p, p * tb, idx_cur, ahead * tb, ahead)
        else:
            phase(p, p * tb, idx_nxt, (ahead - nbuf) * tb, ahead - nbuf)

    @pl.when(i == nb - 1)
    def _():
        for p in range(PEER_AHEAD):
            block_wait(p)


def _peer_mix(idx, gate, uv, h2, x1, mod6, gate_idx, seq, tb=PEER_TB, nbuf=PEER_NBUF):
    n, ne = idx.shape
    d = h2.shape[1]
    half = d // LANES
    ts = nbuf * tb
    nb = n // ts
    assert n % ts == 0 and seq % ts == 0 and ne == LANES and PEER_AHEAD < nbuf
    h3 = h2.reshape(n, half, LANES)
    x3 = x1.reshape(n, half, LANES)
    m3 = mod6.reshape(mod6.shape[0], half, LANES)
    rows = pltpu.VMEM((tb * ne, 2 * half, LANES), BF16)
    out = pl.pallas_call(
        functools.partial(_peer_mix_body, tb=tb, ne=ne, nbuf=nbuf),
        grid=(nb,),
        in_specs=[
            pl.BlockSpec((ts, ne), lambda i: (i, 0), memory_space=pltpu.SMEM),
            pl.BlockSpec((ts, ne), lambda i: (jnp.minimum(i + 1, nb - 1), 0), memory_space=pltpu.SMEM),
            pl.BlockSpec(memory_space=pl.ANY),
            pl.BlockSpec((ts, half, LANES), lambda i: (i, 0, 0)),
            pl.BlockSpec((ts, ne), lambda i: (i, 0)),
            pl.BlockSpec((ts, half, LANES), lambda i: (i, 0, 0)),
            pl.BlockSpec((None, half, LANES), lambda i: ((i * ts // seq) * 6 + gate_idx, 0, 0)),
        ],
        out_specs=pl.BlockSpec((ts, half, LANES), lambda i: (i, 0, 0)),
        out_shape=jax.ShapeDtypeStruct((n, half, LANES), F32),
        scratch_shapes=[rows] * nbuf + [pltpu.VMEM((2, ne, ne), F32), pltpu.SemaphoreType.DMA((nbuf,))],
        compiler_params=_cparams(("arbitrary",)),
        name="peer_mix",
    )(idx, idx, uv, h3, gate, x3, m3)
    return out.reshape(n, d)


def _layer(x, c, norm1_g, norm2_g, w_ada, b_ada, w_in, b_in, q_norm_g, k_norm_g,
           cmp_pos_k, cmp_pos_v, cmp_k_w1, cmp_k_w2, cmp_v_w1, cmp_v_w2, w_nsa_out,
           conv_dw_w, conv_dw_b, conv_ln_g, conv_ln_b, conv_pw_w, conv_pw_b, w_out,
           peer_w_q, peer_sub_keys, peer_u, peer_v):
    b, s, d = x.shape
    n = b * s
    g = N_KV_GROUPS
    conv_ch = conv_dw_w.shape[1]
    x2 = x.reshape(n, d)

    rows = max(SUBLANES, b)
    sc = jnp.zeros((rows, d), F32).at[:b].set(c * _sigmoid(c))
    mod = _matmul_bias(sc, w_ada, b_ada.reshape(1, -1), tm=rows)[:b]
    mod6 = mod.reshape(b * 6, 1, d)

    sizes = (Q_COLS,) + (KV_COLS,) * 6 + (3 * N_HEADS, 2 * conv_ch, 2 * d)
    offs = [0]
    for sz in sizes:
        offs.append(offs[-1] + sz)
    part = lambda arr, k: arr[..., offs[k]:offs[k + 1]]
    per_group = 3 * HEADS_PER_GROUP

    def gate_cols(arr):
        a = part(arr, 7).reshape(arr.shape[:-1] + (g, per_group))
        a = jnp.pad(a, [(0, 0)] * (a.ndim - 1) + [(0, LANES - per_group)])
        return a.reshape(arr.shape[:-1] + (GATE_PAD,))

    order = [0, 8, 9, 1, 2, 3, 4, 5, 6]
    w_cat = jnp.concatenate([part(w_in, k) for k in order] + [gate_cols(w_in)], axis=-1).astype(BF16)
    b_cat = jnp.concatenate([part(b_in, k) for k in order] + [gate_cols(b_in)], axis=-1).reshape(1, -1)
    col_q = 0
    col_glu = Q_COLS
    col_merge = col_glu + 2 * conv_ch
    col_cmp = col_merge + 2 * d
    col_slc = col_cmp + 2 * KV_COLS
    col_win = col_slc + 2 * KV_COLS
    col_gate = col_win + 2 * KV_COLS

    z, = _matmul_norm(x2, norm1_g.reshape(1, d), mod6, 0, 1, s, w_cat, b_cat, want_f32=False, tm=MM_TM)

    qn, kvn = _head_norm(z, col_q, col_slc, col_win, q_norm_g.reshape(1, -1),
                         k_norm_g[1].reshape(1, -1), k_norm_g[2].reshape(1, -1))
    kcn = _compress(z, col_cmp, cmp_pos_k, cmp_k_w1, cmp_k_w2, k_norm_g[0].reshape(1, -1), True, b, s)
    vcm = _compress(z, col_cmp + KV_COLS, cmp_pos_v, cmp_v_w1, cmp_v_w2, k_norm_g[0].reshape(1, -1), False, b, s)

    attn = _nsa(qn, kvn, kcn, vcm, z, col_gate, b, s)
    y_attn = _matmul_bias(attn, w_nsa_out, jnp.zeros((1, d), F32))

    yc = _conv_module(z, col_glu, col_glu + conv_ch, conv_dw_w, conv_dw_b.reshape(1, -1),
                      conv_ln_g.reshape(1, -1), conv_ln_b.reshape(1, -1), b, s)
    merged = _matmul_merge(yc, conv_pw_w, conv_pw_b.reshape(1, -1), z, col_merge, col_merge + d, y_attn)
    x1 = _matmul_resid(merged, w_out, x2, mod6, 2, s)

    qp, h2f = _matmul_norm(x1, norm2_g.reshape(1, d), mod6, 3, 4, s, peer_w_q,
                           jnp.zeros((1, peer_w_q.shape[1]), F32), want_f32=True, tm=MM_TM // 2)
    e_t, g_t = _peer_topk(qp, peer_sub_keys)
    idx = e_t.T
    gate = g_t.T
    uv = _uv_pack(peer_u, peer_v)
    out = _peer_mix(idx, gate, uv, h2f, x1, mod6, 5, s)
    return out.reshape(b, s, d)


def kernel(x, c, norm1_g, norm2_g, w_ada, b_ada, w_in, b_in, q_norm_g, k_norm_g, cmp_pos_k, cmp_pos_v,
           cmp_k_w1, cmp_k_w2, cmp_v_w1, cmp_v_w2, w_nsa_out, conv_dw_w, conv_dw_b, conv_ln_g, conv_ln_b,
           conv_pw_w, conv_pw_b, w_out, peer_w_q, peer_sub_keys, peer_u, peer_v):
    depth = norm1_g.shape[0]
    for l in range(depth):
        x = _layer(x, c, norm1_g[l], norm2_g[l], w_ada[l], b_ada[l], w_in[l], b_in[l],
                   q_norm_g[l], k_norm_g[l], cmp_pos_k[l], cmp_pos_v[l],
                   cmp_k_w1[l], cmp_k_w2[l], cmp_v_w1[l], cmp_v_w2[l], w_nsa_out[l],
                   conv_dw_w[l], conv_dw_b[l], conv_ln_g[l], conv_ln_b[l],
                   conv_pw_w[l], conv_pw_b[l], w_out[l],
                   peer_w_q[l], peer_sub_keys[l], peer_u[l], peer_v[l])
    return x
```

```python
import functools
import math

import jax
import jax.numpy as jnp
from jax import lax
from jax.experimental import pallas as pl
from jax.experimental.pallas import tpu as pltpu

F32 = jnp.float32
BF16 = jnp.bfloat16
I32 = jnp.int32

N_HEADS = 16
N_KV_GROUPS = 4
HEADS_PER_GROUP = N_HEADS // N_KV_GROUPS
HEAD_DIM = 128
L_CMP = 32
STRIDE_CMP = 16
L_SEL = 64
N_SEL = 16
WINDOW = 512
CONV_WIDTH = 31
PEER_HEADS = 8
PEER_NKEYS = 128
PEER_HALF = 128
PEER_TOPK = 16
EPS = 1e-6
NEG_BIG = 1e30

LANES = 128
SUBLANES = 8
VMEM_LIMIT = 48 * 1024 * 1024

MM_TM = 1024
MM_TN = 512
ROW_TILE = 256
NSA_TQ = 256
NSA_TK = 1024
CONV_TS = 256
CONV_HALO = 32
PEER_TM = 1024
PEER_TB = 8
PEER_NBUF = 4
PEER_AHEAD = 2

Q_COLS = N_HEADS * HEAD_DIM
KV_COLS = N_KV_GROUPS * HEAD_DIM
GATE_PAD = N_KV_GROUPS * LANES


def _cparams(sem):
    return pltpu.CompilerParams(dimension_semantics=sem, vmem_limit_bytes=VMEM_LIMIT)


def _gelu_tanh(x):
    c = math.sqrt(2.0 / math.pi)
    return x * (0.5 * (1.0 + jnp.tanh(c * (x + 0.044715 * (x * x * x)))))


def _sigmoid(x):
    return 1.0 / (1.0 + jnp.exp(-x))


def _mm_bias_body(a_ref, b_ref, bias_ref, o_ref):
    acc = jnp.dot(a_ref[...].astype(BF16), b_ref[...].astype(BF16), preferred_element_type=F32)
    o_ref[...] = (acc + bias_ref[...]).astype(o_ref.dtype)


def _matmul_bias(a, b, bias, out_dtype=F32, tm=MM_TM, tn=MM_TN):
    m, k = a.shape
    n = b.shape[1]
    tm = min(tm, m)
    tn = min(tn, n)
    assert m % tm == 0 and n % tn == 0
    return pl.pallas_call(
        _mm_bias_body,
        grid=(m // tm, n // tn),
        in_specs=[
            pl.BlockSpec((tm, k), lambda i, j: (i, 0)),
            pl.BlockSpec((k, tn), lambda i, j: (0, j)),
            pl.BlockSpec((1, tn), lambda i, j: (0, j)),
        ],
        out_specs=pl.BlockSpec((tm, tn), lambda i, j: (i, j)),
        out_shape=jax.ShapeDtypeStruct((m, n), out_dtype),
        compiler_params=_cparams(("parallel", "arbitrary")),
        name="mm_bias",
    )(a, b, bias)


def _mm_merge_body(a_ref, b_ref, bias_ref, g0_ref, g1_ref, ya_ref, o_ref):
    acc = jnp.dot(a_ref[...], b_ref[...].astype(BF16), preferred_element_type=F32) + bias_ref[...]
    o_ref[...] = (_sigmoid(g0_ref[...]) * ya_ref[...] + _sigmoid(g1_ref[...]) * acc).astype(o_ref.dtype)


def _matmul_merge(a, b, bias, z, g0_col, g1_col, y_attn, tm=MM_TM, tn=MM_TN):
    m, k = a.shape
    n = b.shape[1]
    assert m % tm == 0 and n % tn == 0 and g0_col % tn == 0 and g1_col % tn == 0
    return pl.pallas_call(
        _mm_merge_body,
        grid=(m // tm, n // tn),
        in_specs=[
            pl.BlockSpec((tm, k), lambda i, j: (i, 0)),
            pl.BlockSpec((k, tn), lambda i, j: (0, j)),
            pl.BlockSpec((1, tn), lambda i, j: (0, j)),
            pl.BlockSpec((tm, tn), lambda i, j: (i, g0_col // tn + j)),
            pl.BlockSpec((tm, tn), lambda i, j: (i, g1_col // tn + j)),
            pl.BlockSpec((tm, tn), lambda i, j: (i, j)),
        ],
        out_specs=pl.BlockSpec((tm, tn), lambda i, j: (i, j)),
        out_shape=jax.ShapeDtypeStruct((m, n), BF16),
        compiler_params=_cparams(("parallel", "arbitrary")),
        name="mm_merge",
    )(a, b, bias, z, z, y_attn)


def _mm_resid_body(a_ref, b_ref, res_ref, gate_ref, o_ref):
    acc = jnp.dot(a_ref[...], b_ref[...].astype(BF16), preferred_element_type=F32)
    o_ref[...] = res_ref[...] + gate_ref[...] * acc


def _matmul_resid(a, b, res, mod6, gate_idx, seq, tm=MM_TM, tn=MM_TN):
    m, k = a.shape
    n = b.shape[1]
    assert m % tm == 0 and n % tn == 0 and seq % tm == 0
    return pl.pallas_call(
        _mm_resid_body,
        grid=(m // tm, n // tn),
        in_specs=[
            pl.BlockSpec((tm, k), lambda i, j: (i, 0)),
            pl.BlockSpec((k, tn), lambda i, j: (0, j)),
            pl.BlockSpec((tm, tn), lambda i, j: (i, j)),
            pl.BlockSpec((None, 1, tn), lambda i, j: ((i * tm // seq) * 6 + gate_idx, 0, j)),
        ],
        out_specs=pl.BlockSpec((tm, tn), lambda i, j: (i, j)),
        out_shape=jax.ShapeDtypeStruct((m, n), F32),
        compiler_params=_cparams(("parallel", "arbitrary")),
        name="mm_resid",
    )(a, b, res, mod6)


def _mm_norm_body(x_ref, g_ref, shift_ref, scale_ref, b_ref, bias_ref, o_ref, *rest, want_f32):
    hb_ref = rest[-1]

    @pl.when(pl.program_id(1) == 0)
    def _():
        x = x_ref[...]
        y = x * lax.rsqrt(jnp.mean(x * x, axis=-1, keepdims=True) + EPS)
        h = (y * g_ref[...]) * (1.0 + scale_ref[...]) + shift_ref[...]
        hb_ref[...] = h.astype(BF16)
        if want_f32:
            rest[0][...] = h

    acc = jnp.dot(hb_ref[...], b_ref[...].astype(BF16), preferred_element_type=F32)
    o_ref[...] = acc + bias_ref[...]


def _matmul_norm(x2d, g, mod6, shift_idx, scale_idx, seq, b, bias, want_f32, tm, tn=MM_TN):
    m, k = x2d.shape
    n = b.shape[1]
    assert m % tm == 0 and n % tn == 0 and seq % tm == 0
    out_shape = [jax.ShapeDtypeStruct((m, n), F32)]
    out_specs = [pl.BlockSpec((tm, tn), lambda i, j: (i, j))]
    if want_f32:
        out_shape.append(jax.ShapeDtypeStruct((m, k), F32))
        out_specs.append(pl.BlockSpec((tm, k), lambda i, j: (i, 0)))
    return pl.pallas_call(
        functools.partial(_mm_norm_body, want_f32=want_f32),
        grid=(m // tm, n // tn),
        in_specs=[
            pl.BlockSpec((tm, k), lambda i, j: (i, 0)),
            pl.BlockSpec((1, k), lambda i, j: (0, 0)),
            pl.BlockSpec((None, 1, k), lambda i, j: ((i * tm // seq) * 6 + shift_idx, 0, 0)),
            pl.BlockSpec((None, 1, k), lambda i, j: ((i * tm // seq) * 6 + scale_idx, 0, 0)),
            pl.BlockSpec((k, tn), lambda i, j: (0, j)),
            pl.BlockSpec((1, tn), lambda i, j: (0, j)),
        ],
        out_specs=out_specs,
        out_shape=out_shape,
        scratch_shapes=[pltpu.VMEM((tm, k), BF16)],
        compiler_params=_cparams(("parallel", "arbitrary")),
        name="mm_norm",
    )(x2d, g, mod6, mod6, b, bias)


def _head_norm_body(q_ref, sl_ref, wn_ref, gq_ref, gs_ref, gw_ref, qn_ref, kv_ref):
    def rms(x, g):
        y = x * lax.rsqrt(jnp.mean(x * x, axis=-1, keepdims=True) + EPS)
        return y * g

    qscale = HEAD_DIM ** -0.5
    for c in range(N_HEADS):
        sl = slice(c * HEAD_DIM, (c + 1) * HEAD_DIM)
        qn_ref[:, sl] = (rms(q_ref[:, sl], gq_ref[...]) * qscale).astype(BF16)
    for g in range(N_KV_GROUPS):
        sl = slice(g * HEAD_DIM, (g + 1) * HEAD_DIM)
        sv = slice(KV_COLS + g * HEAD_DIM, KV_COLS + (g + 1) * HEAD_DIM)
        kv_ref[:, sl] = rms(sl_ref[:, sl], gs_ref[...]).astype(BF16)
        kv_ref[:, sv] = sl_ref[:, sv].astype(BF16)
        kv_ref[:, slice(2 * KV_COLS + g * HEAD_DIM, 2 * KV_COLS + (g + 1) * HEAD_DIM)] = (
            rms(wn_ref[:, sl], gw_ref[...]).astype(BF16))
        kv_ref[:, slice(3 * KV_COLS + g * HEAD_DIM, 3 * KV_COLS + (g + 1) * HEAD_DIM)] = (
            wn_ref[:, sv].astype(BF16))


def _head_norm(z, col_q, col_slc, col_win, gq, gs, gw, tr=ROW_TILE):
    n = z.shape[0]
    w2 = 2 * KV_COLS
    assert n % tr == 0 and col_q % Q_COLS == 0 and col_slc % w2 == 0 and col_win % w2 == 0
    return pl.pallas_call(
        _head_norm_body,
        grid=(n // tr,),
        in_specs=[
            pl.BlockSpec((tr, Q_COLS), lambda i: (i, col_q // Q_COLS)),
            pl.BlockSpec((tr, w2), lambda i: (i, col_slc // w2)),
            pl.BlockSpec((tr, w2), lambda i: (i, col_win // w2)),
            pl.BlockSpec((1, HEAD_DIM), lambda i: (0, 0)),
            pl.BlockSpec((1, HEAD_DIM), lambda i: (0, 0)),
            pl.BlockSpec((1, HEAD_DIM), lambda i: (0, 0)),
        ],
        out_specs=[
            pl.BlockSpec((tr, Q_COLS), lambda i: (i, 0)),
            pl.BlockSpec((tr, 4 * KV_COLS), lambda i: (i, 0)),
        ],
        out_shape=[
            jax.ShapeDtypeStruct((n, Q_COLS), BF16),
            jax.ShapeDtypeStruct((n, 4 * KV_COLS), BF16),
        ],
        compiler_params=_cparams(("parallel",)),
        name="head_norm",
    )(z, z, z, gq, gs, gw)


def _compress_body(z_ref, pa_ref, pb_ref, wt_ref, wb_ref, w2_ref, g_ref, o_ref, *, normalize):
    n_chunk = z_ref.shape[0] // STRIDE_CMP
    top = jnp.zeros((n_chunk, HEAD_DIM), F32)
    bot = jnp.zeros((n_chunk, HEAD_DIM), F32)
    for l in range(STRIDE_CMP):
        rows = z_ref[pl.ds(l, n_chunk, stride=STRIDE_CMP), :]
        wl = slice(l * HEAD_DIM, (l + 1) * HEAD_DIM)
        top = top + jnp.dot((rows + pa_ref[l:l + 1, :]).astype(BF16), wt_ref[wl, :].astype(BF16),
                            preferred_element_type=F32)
        bot = bot + jnp.dot((rows + pb_ref[l:l + 1, :]).astype(BF16), wb_ref[wl, :].astype(BF16),
                            preferred_element_type=F32)
    pre = top + pltpu.roll(bot, n_chunk - 1, 0)
    out = jnp.dot(_gelu_tanh(pre).astype(BF16), w2_ref[...].astype(BF16), preferred_element_type=F32)
    if normalize:
        out = out * lax.rsqrt(jnp.mean(out * out, axis=-1, keepdims=True) + EPS) * g_ref[...]
    o_ref[...] = out.astype(BF16)


def _compress(z, col, pos, w1, w2, gain, normalize, batch, seq):
    g = N_KV_GROUPS
    n_chunk = seq // STRIDE_CMP
    half = STRIDE_CMP * HEAD_DIM
    assert col % HEAD_DIM == 0 and seq % STRIDE_CMP == 0
    full = lambda *shape: pl.BlockSpec(shape, lambda bi, gi: (0,) * len(shape))
    return pl.pallas_call(
        functools.partial(_compress_body, normalize=normalize),
        grid=(batch, g),
        in_specs=[
            pl.BlockSpec((seq, HEAD_DIM), lambda bi, gi: (bi, col // HEAD_DIM + gi)),
            full(STRIDE_CMP, HEAD_DIM), full(STRIDE_CMP, HEAD_DIM),
            full(half, HEAD_DIM), full(half, HEAD_DIM), full(HEAD_DIM, HEAD_DIM), full(1, HEAD_DIM),
        ],
        out_specs=pl.BlockSpec((None, None, n_chunk, HEAD_DIM), lambda bi, gi: (bi, gi, 0, 0)),
        out_shape=jax.ShapeDtypeStruct((batch, g, n_chunk, HEAD_DIM), BF16),
        compiler_params=_cparams(("parallel", "parallel")),
        name="compress",
    )(z, pos[:STRIDE_CMP], pos[STRIDE_CMP:], w1[:half], w1[half:], w2, gain)


def _nsa_body(q_ref, kc_ref, vc_ref, ks_ref, vs_ref, kw_ref, vw_ref, g_ref, u_ref, v_ref, o_ref, uv_ref,
              qs_ref, sc_ref, oc_ref, *, tq, tk, seq):
    _uv_pack_body(u_ref, v_ref, uv_ref)

    n_cmp = (seq - L_CMP) // STRIDE_CMP + 1
    n_blk = seq // L_SEL
    n_sel = min(N_SEL, n_blk)
    ncp = kc_ref.shape[0]
    hpg = HEADS_PER_GROUP
    wk = WINDOW + tq
    nt_dims = (((1,), (1,)), ((), ()))
    masked = -NEG_BIG
    m_floor = -0.1 * NEG_BIG

    t0 = pl.program_id(2) * tq
    t_col = t0 + lax.broadcasted_iota(I32, (tq, 1), 0)

    for h in range(hpg):
        qs_ref[h * tq:(h + 1) * tq, :] = q_ref[:, h * HEAD_DIM:(h + 1) * HEAD_DIM]
    q_all = qs_ref[...]

    def per_head(x):
        return jnp.concatenate([x] * hpg, axis=0)

    def softmax_terms(s):
        m = jnp.maximum(jnp.max(s, axis=-1, keepdims=True), m_floor)
        e = jnp.exp(s - m)
        return e, jnp.sum(e, axis=-1, keepdims=True)

    gates = _sigmoid(g_ref[...])
    gate_col = lambda c: jnp.concatenate([gates[:, 3 * h + c:3 * h + c + 1] for h in range(hpg)], axis=0)

    n_row = lax.broadcasted_iota(I32, (1, ncp), 1)
    vis = (n_row * STRIDE_CMP + (L_CMP - 1) <= t_col) & (n_row < n_cmp)
    s = lax.dot_general(q_all, kc_ref[...], nt_dims, preferred_element_type=F32)
    e, l = softmax_terms(s + per_head(jnp.where(vis, 0.0, masked)))
    p = e / jnp.maximum(l, 1e-30)
    oc_ref[...] = gate_col(0) * jnp.dot(p.astype(BF16), vc_ref[...], preferred_element_type=F32)
    psum = p[0:tq]
    for h in range(1, hpg):
        psum = psum + p[h * tq:(h + 1) * tq]

    w0 = pl.multiple_of(jnp.minimum(jnp.maximum(t0 - WINDOW, 0), seq - wk), tq)
    dist = t_col - (w0 + lax.broadcasted_iota(I32, (1, wk), 1))
    bias_w = jnp.where(dist >= 0, jnp.where(dist < WINDOW, 0.0, masked), masked)
    s = lax.dot_general(q_all, kw_ref[pl.ds(w0, wk), :], nt_dims, preferred_element_type=F32)
    e, l = softmax_terms(s + per_head(bias_w))
    o_win = jnp.dot(e.astype(BF16), vw_ref[pl.ds(w0, wk), :], preferred_element_type=F32)
    oc_ref[...] = oc_ref[...] + gate_col(2) * (o_win / l)

    j_row = lax.broadcasted_iota(I32, (1, LANES), 1)
    n_colv = lax.broadcasted_iota(I32, (ncp, 1), 0)
    overlap = ((n_colv * STRIDE_CMP < j_row * L_SEL + L_SEL)
               & (n_colv * STRIDE_CMP + L_CMP > j_row * L_SEL)
               & (n_colv < n_cmp) & (j_row < n_blk))
    overlap = jnp.where(overlap, 1.0, 0.0).astype(BF16)
    p_hi = psum.astype(BF16)
    r1 = psum - p_hi.astype(F32)
    p_mid = r1.astype(BF16)
    p_lo = (r1 - p_mid.astype(F32)).astype(BF16)
    imp = (jnp.dot(p_hi, overlap, preferred_element_type=F32)
           + jnp.dot(p_mid, overlap, preferred_element_type=F32)
           + jnp.dot(p_lo, overlap, preferred_element_type=F32))

    sel_shift = L_SEL.bit_length() - 1
    cur = jnp.right_shift(t_col, sel_shift)
    forced = (j_row == 0) | (j_row == cur) | (j_row == cur - 1)
    score = jnp.where(forced, NEG_BIG, jnp.where(j_row * L_SEL <= t_col, imp, -NEG_BIG))
    sc_ref[...] = jnp.transpose(score)
    n_grp = n_blk // SUBLANES
    groups = [sc_ref[g * SUBLANES:(g + 1) * SUBLANES, :] for g in range(n_grp)]
    j_sub = lax.broadcasted_iota(I32, (SUBLANES, tq), 0)
    ranks = [jnp.zeros((SUBLANES, tq), F32) for _ in range(n_grp)]
    for i in range(n_blk):
        rb = jnp.broadcast_to(sc_ref[i:i + 1, :], (SUBLANES, tq))
        for g in range(n_grp):
            lo = g * SUBLANES
            if lo > i:
                beats = jnp.where(rb >= groups[g], 1.0, 0.0)
            elif lo + SUBLANES - 1 <= i:
                beats = jnp.where(rb > groups[g], 1.0, 0.0)
            else:
                beats = jnp.where(j_sub + lo > i, jnp.where(rb >= groups[g], 1.0, 0.0),
                                  jnp.where(rb > groups[g], 1.0, 0.0))
            ranks[g] = ranks[g] + beats
    sel_t = jnp.concatenate([jnp.where(r < n_sel, 1.0, 0.0) for r in ranks]
                            + [jnp.zeros((LANES - n_blk, tq), F32)], axis=0)
    sel = jnp.transpose(sel_t).astype(BF16)

    c_row = lax.broadcasted_iota(I32, (1, tk), 1)
    j_colv = lax.broadcasted_iota(I32, (LANES, 1), 0)
    ones_cols = jnp.ones((tk, HEAD_DIM), BF16)
    rows = hpg * tq

    def slc_block(jb, state):
        m_old, l_old, acc_old = state
        k0 = jb * tk
        kpos = k0 + c_row
        expand = jnp.where(j_colv == jnp.right_shift(kpos, sel_shift), 1.0, 0.0).astype(BF16)
        chosen = jnp.dot(sel, expand, preferred_element_type=F32)
        bias = jnp.where(kpos <= t_col, jnp.where(chosen > 0.5, 0.0, masked), masked)
        s = lax.dot_general(q_all, ks_ref[k0:k0 + tk, :], nt_dims, preferred_element_type=F32)
        s = s + per_head(bias)
        m_new = jnp.maximum(m_old, jnp.max(s, axis=-1, keepdims=True))
        p = jnp.exp(s - m_new).astype(BF16)
        alpha = jnp.broadcast_to(jnp.exp(m_old - m_new), (rows, HEAD_DIM))
        pv = jnp.dot(p, jnp.concatenate([vs_ref[k0:k0 + tk, :], ones_cols], axis=1), preferred_element_type=F32)
        return m_new, alpha * l_old + pv[:, HEAD_DIM:], alpha * acc_old + pv[:, :HEAD_DIM]

    n_blocks = lax.div(t0 + (tq + tk - 1), tk)
    for nb in range(1, seq // tk + 1):
        @pl.when(n_blocks == nb)
        def _(nb=nb):
            state = (jnp.full((rows, 1), m_floor, F32), jnp.zeros((rows, HEAD_DIM), F32),
                     jnp.zeros((rows, HEAD_DIM), F32))
            for jb in range(nb):
                state = slc_block(jb, state)
            out = oc_ref[...] + gate_col(1) * (state[2] / state[1])
            for h in range(hpg):
                o_ref[:, h * HEAD_DIM:(h + 1) * HEAD_DIM] = out[h * tq:(h + 1) * tq].astype(BF16)


def _nsa(qn, kvn, kcn, vcm, z, col_gate, peer_u, peer_v, batch, seq, tq=NSA_TQ, tk=NSA_TK):
    n = qn.shape[0]
    g = N_KV_GROUPS
    gw = HEADS_PER_GROUP * HEAD_DIM
    nq = seq // tq
    ncp = kcn.shape[2]
    rows = HEADS_PER_GROUP * tq
    n_exp, d_exp = peer_u.shape
    te = n_exp // (batch * g * nq)
    half = d_exp // LANES
    assert seq % tq == 0 and seq % tk == 0 and col_gate % LANES == 0
    assert seq // L_SEL <= LANES and (seq // L_SEL) % SUBLANES == 0 and seq >= WINDOW + tq
    assert te * batch * g * nq == n_exp and te % SUBLANES == 0 and d_exp % LANES == 0
    kv_spec = lambda part: pl.BlockSpec((seq, HEAD_DIM), lambda b, gi, i: (b, part * g + gi))
    cmp_spec = pl.BlockSpec((None, None, ncp, HEAD_DIM), lambda b, gi, i: (b, gi, 0, 0))
    step = lambda b, gi, i: (b * g + gi) * nq + i
    return pl.pallas_call(
        functools.partial(_nsa_body, tq=tq, tk=tk, seq=seq),
        grid=(batch, g, nq),
        in_specs=[
            pl.BlockSpec((tq, gw), lambda b, gi, i: (b * nq + i, gi)),
            cmp_spec, cmp_spec,
            kv_spec(0), kv_spec(1), kv_spec(2), kv_spec(3),
            pl.BlockSpec((tq, LANES), lambda b, gi, i: (b * nq + i, col_gate // LANES + gi)),
            pl.BlockSpec((te, d_exp), lambda b, gi, i: (step(b, gi, i), 0)),
            pl.BlockSpec((te, d_exp), lambda b, gi, i: (step(b, gi, i), 0)),
        ],
        out_specs=[
            pl.BlockSpec((tq, gw), lambda b, gi, i: (b * nq + i, gi)),
            pl.BlockSpec((te, 2 * half, LANES), lambda b, gi, i: (step(b, gi, i), 0, 0)),
        ],
        out_shape=[
            jax.ShapeDtypeStruct((n, Q_COLS), BF16),
            jax.ShapeDtypeStruct((n_exp, 2 * half, LANES), BF16),
        ],
        scratch_shapes=[pltpu.VMEM((rows, HEAD_DIM), BF16), pltpu.VMEM((LANES, tq), F32),
                        pltpu.VMEM((rows, HEAD_DIM), F32)],
        compiler_params=_cparams(("parallel", "parallel", "arbitrary")),
        name="nsa",
    )(qn, kcn, vcm, kvn, kvn, kvn, kvn, z, peer_u, peer_v)


def _conv_body(u_ref, v_ref, hu_ref, hv_ref, w_ref, b_ref, lg_ref, lb_ref, o_ref, ubuf, ybuf, *, ts, halo):
    i = pl.program_id(1)
    hal = hu_ref[...] * _sigmoid(hv_ref[...])
    ubuf[0:halo, :] = jnp.where(i > 0, hal, 0.0)
    ubuf[halo:halo + ts, :] = u_ref[...] * _sigmoid(v_ref[...])
    ch = u_ref.shape[1]
    off = halo - (CONV_WIDTH - 1)
    for c in range(ch // LANES):
        sl = slice(c * LANES, (c + 1) * LANES)
        acc = jnp.zeros((ts, LANES), F32)
        for k in range(CONV_WIDTH):
            acc = acc + ubuf[off + k:off + k + ts, sl] * w_ref[k:k + 1, sl]
        ybuf[:, sl] = acc + b_ref[:, sl]
    y = ybuf[...]
    mu = jnp.mean(y, axis=-1, keepdims=True)
    yc = y - mu
    yn = yc * lax.rsqrt(jnp.mean(yc * yc, axis=-1, keepdims=True) + EPS) * lg_ref[...] + lb_ref[...]
    o_ref[...] = (yn * _sigmoid(yn)).astype(BF16)


def _conv_module(z, col_u, col_v, dw_w, dw_b, ln_g, ln_b, batch, seq, ts=CONV_TS, halo=CONV_HALO):
    n = z.shape[0]
    ch = dw_w.shape[1]
    ns = seq // ts
    r = ts // halo
    assert seq % ts == 0 and ts % halo == 0 and col_u % ch == 0 and col_v % ch == 0
    halo_map = lambda col: (lambda b, i: (jnp.maximum((b * ns + i) * r - 1, 0), col // ch))
    full = lambda *shape: pl.BlockSpec(shape, lambda b, i: (0,) * len(shape))
    return pl.pallas_call(
        functools.partial(_conv_body, ts=ts, halo=halo),
        grid=(batch, ns),
        in_specs=[
            pl.BlockSpec((ts, ch), lambda b, i: (b * ns + i, col_u // ch)),
            pl.BlockSpec((ts, ch), lambda b, i: (b * ns + i, col_v // ch)),
            pl.BlockSpec((halo, ch), halo_map(col_u)),
            pl.BlockSpec((halo, ch), halo_map(col_v)),
            full(CONV_WIDTH, ch), full(1, ch), full(1, ch), full(1, ch),
        ],
        out_specs=pl.BlockSpec((ts, ch), lambda b, i: (b * ns + i, 0)),
        out_shape=jax.ShapeDtypeStruct((n, ch), BF16),
        scratch_shapes=[pltpu.VMEM((ts + halo, ch), F32), pltpu.VMEM((ts, ch), F32)],
        compiler_params=_cparams(("parallel", "arbitrary")),
        name="conv_module",
    )(z, z, z, z, dw_w, dw_b, ln_g, ln_b)


def _peer_topk_body(q_ref, sk_ref, e_ref, g_ref, work, v1, i1, v2, i2, cand, candi, ts_buf, te_buf):
    k = PEER_TOPK
    tmn = q_ref.shape[0]
    nt_dims = (((1,), (1,)), ((), ()))

    def extract(src, nrows, val_out, idx_out, payload=None):
        rows = lax.broadcasted_iota(I32, (nrows, tmn), 0).astype(F32)

        def body(r, carry):
            x = src[0:nrows, :]
            m = jnp.max(x, axis=0, keepdims=True)
            pos = jnp.min(jnp.where(x == m, rows, float(nrows)), axis=0, keepdims=True)
            hit = rows == pos
            val_out[pl.ds(r, 1), :] = m
            if payload is None:
                idx_out[pl.ds(r, 1), :] = pos
            else:
                idx_out[pl.ds(r, 1), :] = jnp.max(jnp.where(hit, payload[...], -1.0), axis=0, keepdims=True)
            src[0:nrows, :] = jnp.where(hit, -jnp.inf, x)
            return carry

        lax.fori_loop(0, k, body, 0)

    for p, (vo, io) in enumerate(((v1, i1), (v2, i2))):
        qh = q_ref[:, p * PEER_HALF:(p + 1) * PEER_HALF].astype(BF16)
        work[0:PEER_NKEYS, :] = lax.dot_general(sk_ref[p].astype(BF16), qh, nt_dims,
                                                preferred_element_type=F32)
        extract(work, PEER_NKEYS, vo, io)

    off = 0
    for a in range(k):
        w = k // (a + 1)
        cand[off:off + w, :] = v1[a:a + 1, :] + v2[0:w, :]
        candi[off:off + w, :] = i1[a:a + 1, :] * float(PEER_NKEYS) + i2[0:w, :]
        off += w
    n_rows = cand.shape[0]
    cand[off:n_rows, :] = jnp.full((n_rows - off, tmn), -jnp.inf, F32)
    candi[off:n_rows, :] = jnp.zeros((n_rows - off, tmn), F32)
    extract(cand, n_rows, ts_buf, te_buf, payload=candi)

    ts = ts_buf[...]
    ex = jnp.exp(ts - jnp.max(ts, axis=0, keepdims=True))
    g_ref[...] = ex / jnp.sum(ex, axis=0, keepdims=True)
    e_ref[...] = te_buf[...].astype(I32)


def _peer_topk(qp, sub_keys, tmn=PEER_TM):
    n = qp.shape[0]
    k = PEER_TOPK
    qd = 2 * PEER_HALF
    sk = sub_keys.reshape(PEER_HEADS, 2, PEER_NKEYS, PEER_HALF)
    assert n % tmn == 0
    rows = PEER_HEADS * k
    n_cand = sum(k // (a + 1) for a in range(k))
    cand_rows = -(-n_cand // SUBLANES) * SUBLANES
    small_f = pltpu.VMEM((k, tmn), F32)
    return pl.pallas_call(
        _peer_topk_body,
        grid=(n // tmn, PEER_HEADS),
        in_specs=[
            pl.BlockSpec((tmn, qd), lambda i, h: (i, h)),
            pl.BlockSpec((None, 2, PEER_NKEYS, PEER_HALF), lambda i, h: (h, 0, 0, 0)),
        ],
        out_specs=[
            pl.BlockSpec((k, tmn), lambda i, h: (h, i)),
            pl.BlockSpec((k, tmn), lambda i, h: (h, i)),
        ],
        out_shape=[
            jax.ShapeDtypeStruct((rows, n), I32),
            jax.ShapeDtypeStruct((rows, n), F32),
        ],
        scratch_shapes=[
            pltpu.VMEM((PEER_NKEYS, tmn), F32),
            small_f, small_f, small_f, small_f,
            pltpu.VMEM((cand_rows, tmn), F32), pltpu.VMEM((cand_rows, tmn), F32),
            small_f, small_f,
        ],
        compiler_params=_cparams(("parallel", "arbitrary")),
        name="peer_topk",
    )(qp, sk)


def _uv_pack_body(u_ref, v_ref, o_ref):
    te, d = u_ref.shape
    half = d // LANES
    o_ref[:, 0:half, :] = u_ref[...].reshape(te, half, LANES).astype(BF16)
    o_ref[:, half:2 * half, :] = v_ref[...].reshape(te, half, LANES).astype(BF16)


def _peer_mix_body(idx_cur, idx_nxt, uv_hbm, h_ref, gate_ref, x_ref, g2_ref, o_ref, *scratch, tb, ne, nbuf):
    bufs, cbuf, sem = scratch[:nbuf], scratch[nbuf], scratch[nbuf + 1]
    i = pl.program_id(0)
    nb = pl.num_programs(0)
    half = h_ref.shape[1]

    def row_copy(e_idx, which, row):
        return pltpu.make_async_copy(uv_hbm.at[e_idx], bufs[which].at[row], sem.at[which])

    def block_wait(which):
        pltpu.make_async_copy(uv_hbm.at[pl.ds(0, tb * ne)], bufs[which], sem.at[which]).wait()

    def issue(idx_ref, row0, which, t):
        for e in range(ne):
            row_copy(idx_ref[row0 + t, e], which, t * ne + e).start(priority=e % 2)

    @pl.when(i == 0)
    def _():
        for p in range(PEER_AHEAD):
            def issue_first(t, carry, p=p):
                issue(idx_cur, p * tb, p, t)
                return carry
            lax.fori_loop(0, tb, issue_first, 0)

    lane = lax.broadcasted_iota(I32, (SUBLANES, LANES), 1)

    def phase(which, tok0, idx_ref, idx_row0, dst):
        src = bufs[which]

        def u_side(t):
            h = h_ref[tok0 + t]
            acc = jnp.zeros((SUBLANES, LANES), F32)
            for e in range(ne):
                prod = src[t * ne + e, 0:half, :].astype(F32) * h
                part = prod[0:SUBLANES]
                for s in range(1, half // SUBLANES):
                    part = part + prod[s * SUBLANES:(s + 1) * SUBLANES]
                acc = jnp.where(lane == e, jnp.sum(part, axis=-1, keepdims=True), acc)
            a = jnp.sum(acc, axis=0, keepdims=True)
            coef = _gelu_tanh(a) * gate_ref[pl.ds(tok0 + t, 1), :]
            cbuf[t % 2] = jnp.transpose(jnp.broadcast_to(coef, (ne, ne)))

        def v_side(t):
            out = jnp.zeros((half, LANES), F32)
            for e in range(ne):
                out = out + cbuf[t % 2, e:e + 1, :] * src[t * ne + e, half:2 * half, :].astype(F32)
            o_ref[tok0 + t] = x_ref[tok0 + t] + g2_ref[...] * out

        block_wait(which)
        u_side(0)
        for t in range(tb - 1):
            issue(idx_ref, idx_row0, dst, t)
            v_side(t)
            u_side(t + 1)
        issue(idx_ref, idx_row0, dst, tb - 1)
        v_side(tb - 1)

    for p in range(nbuf):
        ahead = p + PEER_AHEAD
        if ahead < nbuf:
            phase(p, p * tb, idx_cur, ahead * tb, ahead)
        else:
            phase(p, p * tb, idx_nxt, (ahead - nbuf) * tb, ahead - nbuf)

    @pl.when(i == nb - 1)
    def _():
        for p in range(PEER_AHEAD):
            block_wait(p)


def _peer_mix(idx, gate, uv, h2, x1, mod6, gate_idx, seq, tb=PEER_TB, nbuf=PEER_NBUF):
    n, ne = idx.shape
    d = h2.shape[1]
    half = d // LANES
    ts = nbuf * tb
    nb = n // ts
    assert n % ts == 0 and seq % ts == 0 and ne == LANES and PEER_AHEAD < nbuf
    h3 = h2.reshape(n, half, LANES)
    x3 = x1.reshape(n, half, LANES)
    m3 = mod6.reshape(mod6.shape[0], half, LANES)
    rows = pltpu.VMEM((tb * ne, 2 * half, LANES), BF16)
    out = pl.pallas_call(
        functools.partial(_peer_mix_body, tb=tb, ne=ne, nbuf=nbuf),
        grid=(nb,),
        in_specs=[
            pl.BlockSpec((ts, ne), lambda i: (i, 0), memory_space=pltpu.SMEM),
            pl.BlockSpec((ts, ne), lambda i: (jnp.minimum(i + 1, nb - 1), 0), memory_space=pltpu.SMEM),
            pl.BlockSpec(memory_space=pl.ANY),
            pl.BlockSpec((ts, half, LANES), lambda i: (i, 0, 0)),
            pl.BlockSpec((ts, ne), lambda i: (i, 0)),
            pl.BlockSpec((ts, half, LANES), lambda i: (i, 0, 0)),
            pl.BlockSpec((None, half, LANES), lambda i: ((i * ts // seq) * 6 + gate_idx, 0, 0)),
        ],
        out_specs=pl.BlockSpec((ts, half, LANES), lambda i: (i, 0, 0)),
        out_shape=jax.ShapeDtypeStruct((n, half, LANES), F32),
        scratch_shapes=[rows] * nbuf + [pltpu.VMEM((2, ne, ne), F32), pltpu.SemaphoreType.DMA((nbuf,))],
        compiler_params=_cparams(("arbitrary",)),
        name="peer_mix",
    )(idx, idx, uv, h3, gate, x3, m3)
    return out.reshape(n, d)


def _layer(x, c, norm1_g, norm2_g, w_ada, b_ada, w_in, b_in, q_norm_g, k_norm_g,
           cmp_pos_k, cmp_pos_v, cmp_k_w1, cmp_k_w2, cmp_v_w1, cmp_v_w2, w_nsa_out,
           conv_dw_w, conv_dw_b, conv_ln_g, conv_ln_b, conv_pw_w, conv_pw_b, w_out,
           peer_w_q, peer_sub_keys, peer_u, peer_v):
    b, s, d = x.shape
    n = b * s
    g = N_KV_GROUPS
    conv_ch = conv_dw_w.shape[1]
    x2 = x.reshape(n, d)

    rows = max(SUBLANES, b)
    sc = jnp.zeros((rows, d), F32).at[:b].set(c * _sigmoid(c))
    mod = _matmul_bias(sc, w_ada, b_ada.reshape(1, -1), tm=rows)[:b]
    mod6 = mod.reshape(b * 6, 1, d)

    sizes = (Q_COLS,) + (KV_COLS,) * 6 + (3 * N_HEADS, 2 * conv_ch, 2 * d)
    offs = [0]
    for sz in sizes:
        offs.append(offs[-1] + sz)
    part = lambda arr, k: arr[..., offs[k]:offs[k + 1]]
    per_group = 3 * HEADS_PER_GROUP

    def gate_cols(arr):
        a = part(arr, 7).reshape(arr.shape[:-1] + (g, per_group))
        a = jnp.pad(a, [(0, 0)] * (a.ndim - 1) + [(0, LANES - per_group)])
        return a.reshape(arr.shape[:-1] + (GATE_PAD,))

    order = [0, 8, 9, 1, 2, 3, 4, 5, 6]
    w_cat = jnp.concatenate([part(w_in, k) for k in order] + [gate_cols(w_in)], axis=-1).astype(BF16)
    b_cat = jnp.concatenate([part(b_in, k) for k in order] + [gate_cols(b_in)], axis=-1).reshape(1, -1)
    col_q = 0
    col_glu = Q_COLS
    col_merge = col_glu + 2 * conv_ch
    col_cmp = col_merge + 2 * d
    col_slc = col_cmp + 2 * KV_COLS
    col_win = col_slc + 2 * KV_COLS
    col_gate = col_win + 2 * KV_COLS

    z, = _matmul_norm(x2, norm1_g.reshape(1, d), mod6, 0, 1, s, w_cat, b_cat, want_f32=False, tm=MM_TM)

    qn, kvn = _head_norm(z, col_q, col_slc, col_win, q_norm_g.reshape(1, -1),
                         k_norm_g[1].reshape(1, -1), k_norm_g[2].reshape(1, -1))
    kcn = _compress(z, col_cmp, cmp_pos_k, cmp_k_w1, cmp_k_w2, k_norm_g[0].reshape(1, -1), True, b, s)
    vcm = _compress(z, col_cmp + KV_COLS, cmp_pos_v, cmp_v_w1, cmp_v_w2, k_norm_g[0].reshape(1, -1), False, b, s)

    attn, uv = _nsa(qn, kvn, kcn, vcm, z, col_gate, peer_u, peer_v, b, s)
    y_attn = _matmul_bias(attn, w_nsa_out, jnp.zeros((1, d), F32))

    yc = _conv_module(z, col_glu, col_glu + conv_ch, conv_dw_w, conv_dw_b.reshape(1, -1),
                      conv_ln_g.reshape(1, -1), conv_ln_b.reshape(1, -1), b, s)
    merged = _matmul_merge(yc, conv_pw_w, conv_pw_b.reshape(1, -1), z, col_merge, col_merge + d, y_attn)
    x1 = _matmul_resid(merged, w_out, x2, mod6, 2, s)

    qp, h2f = _matmul_norm(x1, norm2_g.reshape(1, d), mod6, 3, 4, s, peer_w_q,
                           jnp.zeros((1, peer_w_q.shape[1]), F32), want_f32=True, tm=MM_TM // 2)
    e_t, g_t = _peer_topk(qp, peer_sub_keys)
    idx = e_t.T
    gate = g_t.T
    out = _peer_mix(idx, gate, uv, h2f, x1, mod6, 5, s)
    return out.reshape(b, s, d)


def kernel(x, c, norm1_g, norm2_g, w_ada, b_ada, w_in, b_in, q_norm_g, k_norm_g, cmp_pos_k, cmp_pos_v,
           cmp_k_w1, cmp_k_w2, cmp_v_w1, cmp_v_w2, w_nsa_out, conv_dw_w, conv_dw_b, conv_ln_g, conv_ln_b,
           conv_pw_w, conv_pw_b, w_out, peer_w_q, peer_sub_keys, peer_u, peer_v):
    depth = norm1_g.shape[0]
    for l in range(depth):
        x = _layer(x, c, norm1_g[l], norm2_g[l], w_ada[l], b_ada[l], w_in[l], b_in[l],
                   q_norm_g[l], k_norm_g[l], cmp_pos_k[l], cmp_pos_v[l],
                   cmp_k_w1[l], cmp_k_w2[l], cmp_v_w1[l], cmp_v_w2[l], w_nsa_out[l],
                   conv_dw_w[l], conv_dw_b[l], conv_ln_g[l], conv_ln_b[l],
                   conv_pw_w[l], conv_pw_b[l], w_out[l],
                   peer_w_q[l], peer_sub_keys[l], peer_u[l], peer_v[l])
    return x
```
